```python
import math
import jax
import jax.numpy as jnp
from jax import lax
import numpy as np

D_MODEL = 1024
BATCH = 8
SEQ = 4096
DEPTH = 2
DEC_BATCH = 8
DEC_SEQ = 32
PAST_LEN = 2048

CHUNK = 64
NORM_EPS = 1e-5
F32 = jnp.float32

SSD_INNER = D_MODEL
SSD_HEAD_DIM = 64
SSD_HEADS = SSD_INNER // SSD_HEAD_DIM
SSD_GROUPS = 2
SSD_HPG = SSD_HEADS // SSD_GROUPS
SSD_STATE = 128
SSD_CONV = 4
SSD_CONV_DIM = SSD_INNER + 2 * SSD_GROUPS * SSD_STATE

GLA_HEADS = 4
GLA_KEY_DIM = D_MODEL // 2
GLA_VAL_DIM = D_MODEL
GLA_DK = GLA_KEY_DIM // GLA_HEADS
GLA_DV = GLA_VAL_DIM // GLA_HEADS
GLA_GATE_RANK = 16
GLA_GATE_NORMALIZER = 16.0

RWKV_HEAD = 64
RWKV_DIM = D_MODEL
RWKV_HEADS = RWKV_DIM // RWKV_HEAD
RWKV_DECAY_RANK = 64
RWKV_ICLR_RANK = 64
RWKV_SHIFT_COLS = 3 * RWKV_DIM + RWKV_DECAY_RANK + RWKV_ICLR_RANK
RWKV_LN_EPS = 64e-5

MEM_LEN = 256
XA_HEADS = 4
XA_HEAD_DIM = D_MODEL // XA_HEADS

N_BRANCHES = 3
IN_SPLITS = (SSD_INNER, SSD_CONV_DIM, SSD_HEADS,
             GLA_KEY_DIM, GLA_KEY_DIM, GLA_VAL_DIM, GLA_VAL_DIM, GLA_GATE_RANK,
             RWKV_SHIFT_COLS, RWKV_DIM,
             N_BRANCHES * D_MODEL)
IN_COLS = sum(IN_SPLITS)

kernel_name = 'hybrid_ssd_gla_rwkv7_stream_step'


def _split(t, sizes):
    idx = []
    acc = 0
    for s in sizes[:-1]:
        acc += s
        idx.append(acc)
    return jnp.split(t, idx, axis=-1)


def _rmsnorm(x, g):
    xf = x.astype(F32)
    y = xf * lax.rsqrt(jnp.mean(xf * xf, axis=-1, keepdims=True) + NORM_EPS)
    return (y * g.astype(F32)).astype(x.dtype)


def _gated_group_rmsnorm(y, z, g, groups):
    b, L, W = y.shape
    t = (y * jax.nn.silu(z)).astype(F32).reshape(b, L, groups, W // groups)
    t = t * lax.rsqrt(jnp.mean(t * t, axis=-1, keepdims=True) + NORM_EPS)
    return (t.reshape(b, L, W) * g.astype(F32)).astype(y.dtype)


def _head_rmsnorm(x, g):
    xf = x.astype(F32)
    y = xf * lax.rsqrt(jnp.mean(xf * xf, axis=-1, keepdims=True) + NORM_EPS)
    return (y * g.astype(F32)).astype(x.dtype)


def _head_layernorm(x, g, bias):
    b, L, H, N = x.shape
    xf = x.astype(F32)
    mu = jnp.mean(xf, axis=-1, keepdims=True)
    var = jnp.mean(jnp.square(xf - mu), axis=-1, keepdims=True)
    y = ((xf - mu) * lax.rsqrt(var + RWKV_LN_EPS)).reshape(b, L, H * N)
    return (y * g.astype(F32) + bias.astype(F32)).astype(x.dtype)


def _causal_dwconv(x, buf, w, bias):
    L, C = x.shape[1], x.shape[2]
    xf = jnp.concatenate([buf, x], axis=1)
    out = lax.conv_general_dilated(xf, w[:, None, :], window_strides=(1,), padding='VALID',
                                   dimension_numbers=('NWC', 'WIO', 'NWC'), feature_group_count=C)
    return out + bias, xf[:, L:]


def _ssd_chunked(x, dt, A, Bm, Cm, h0):
    b, L = x.shape[0], x.shape[1]
    cs = min(CHUNK, L)
    nc = L // cs
    G, E, P, N = SSD_GROUPS, SSD_HPG, SSD_HEAD_DIM, SSD_STATE
    dtype = x.dtype
    xc = (x * dt.astype(dtype)[..., None]).reshape(b, nc, cs, G, E, P)
    Bc = Bm.reshape(b, nc, cs, G, N)
    Cc = Cm.reshape(b, nc, cs, G, N)
    acs = jnp.cumsum((dt * A).reshape(b, nc, cs, G, E), axis=2)
    causal = jnp.tril(jnp.ones((cs, cs), bool))[:, :, None, None]
    seg = acs[:, :, :, None] - acs[:, :, None, :]
    Lmat = jnp.exp(jnp.where(causal, seg, -jnp.inf)).astype(dtype)
    CB = jnp.einsum('bclgn,bcsgn->bclsg', Cc, Bc)
    y_diag = jnp.einsum('bclsg,bclsge,bcsgep->bclgep', CB, Lmat, xc)
    to_end = jnp.exp(acs[:, :, -1:] - acs).astype(dtype)
    chunk_states = jnp.einsum('bclgn,bclge,bclgep->bcgepn', Bc, to_end, xc)
    chunk_decay = jnp.exp(acs[:, :, -1])

    def step(h, inp):
        st, dec = inp
        return (h * dec[..., None, None] + st).astype(h.dtype), h

    hT, h_in = lax.scan(step, h0.reshape(b, G, E, P, N),
                        (jnp.moveaxis(chunk_states, 1, 0), jnp.moveaxis(chunk_decay, 1, 0)))
    h_in = jnp.moveaxis(h_in, 0, 1).astype(dtype)
    y_off = jnp.einsum('bclgn,bcgepn,bclge->bclgep', Cc, h_in, jnp.exp(acs).astype(dtype))
    y = (y_diag + y_off).reshape(b, L, G * E, P)
    return y, hT.reshape(b, G * E, P, N)


def _gla_chunked(q, k, v, glog, h0):
    b, L, H, K = q.shape
    V = v.shape[-1]
    cs = min(CHUNK, L)
    nc = L // cs
    dtype = q.dtype
    qc = q.reshape(b, nc, cs, H, K) * (K ** -0.5)
    kc = k.reshape(b, nc, cs, H, K)
    vc = v.reshape(b, nc, cs, H, V)
    gcs = jnp.cumsum(glog.reshape(b, nc, cs, H, K), axis=2)
    q_e = (qc * jnp.exp(gcs)).astype(dtype)
    k_e = (kc * jnp.exp(-gcs)).astype(dtype)
    causal = jnp.tril(jnp.ones((cs, cs), bool))
    A = jnp.einsum('bclhk,bcshk->bchls', q_e, k_e)
    A = jnp.where(causal, A, 0).astype(dtype)
    o_intra = jnp.einsum('bchls,bcshv->bclhv', A, vc)
    k_end = (kc * jnp.exp(gcs[:, :, -1:] - gcs)).astype(dtype)
    chunk_states = jnp.einsum('bclhk,bclhv->bchkv', k_end, vc)
    chunk_decay = jnp.exp(gcs[:, :, -1])

    def step(hs, inp):
        st, dec = inp
        return (hs * dec[..., None] + st).astype(hs.dtype), hs

    hT, h_in = lax.scan(step, h0, (jnp.moveaxis(chunk_states, 1, 0), jnp.moveaxis(chunk_decay, 1, 0)))
    h_in = jnp.moveaxis(h_in, 0, 1).astype(dtype)
    o_inter = jnp.einsum('bclhk,bchkv->bclhv', q_e, h_in)
    return (o_intra + o_inter).reshape(b, L, H, V), hT


def _rwkv7_scan(r, w, k, v, kk, a, S0):
    def step(S, inp):
        r_t, w_t, k_t, v_t, kk_t, a_t = inp
        s_kk = jnp.einsum('bhvk,bhk->bhv', S, kk_t)
        S_new = (S * w_t[:, :, None, :] - s_kk[..., None] * (kk_t * a_t)[:, :, None, :]
                 + v_t[..., None] * k_t[:, :, None, :]).astype(S.dtype)
        return S_new, jnp.einsum('bhvk,bhk->bhv', S_new, r_t)

    xs = tuple(jnp.moveaxis(t, 1, 0) for t in (r, w, k, v, kk, a))
    ST, o = lax.scan(step, S0, xs)
    return jnp.moveaxis(o, 0, 1), ST


def _mixer(u, lp, ssd_h, conv_buf, gla_h, rwkv_h, shift_buf):
    b, L, _ = u.shape
    dtype = u.dtype
    proj = jnp.einsum('bld,dc->blc', u, lp['w_in'])
    (z, xbc, dt_raw, gq, gk, gv, ggate, glr, rf, rgate, merge) = _split(proj, IN_SPLITS)

    xbc, conv_new = _causal_dwconv(xbc, conv_buf, lp['ssd_conv_w'], lp['ssd_conv_b'])
    xbc = jax.nn.silu(xbc)
    xs, Bm, Cm = _split(xbc, (SSD_INNER, SSD_GROUPS * SSD_STATE, SSD_GROUPS * SSD_STATE))
    xs = xs.reshape(b, L, SSD_HEADS, SSD_HEAD_DIM)
    Bm = Bm.reshape(b, L, SSD_GROUPS, SSD_STATE)
    Cm = Cm.reshape(b, L, SSD_GROUPS, SSD_STATE)
    dt = jax.nn.softplus(dt_raw.astype(F32) + lp['ssd_dt_bias'].astype(F32))
    A = -jnp.exp(lp['ssd_A_log'].astype(F32))
    y, ssd_new = _ssd_chunked(xs, dt, A, Bm, Cm, ssd_h)
    y = y + xs * lp['ssd_D'][:, None]
    o_ssd = _gated_group_rmsnorm(y.reshape(b, L, SSD_INNER), z, lp['ssd_norm'], SSD_GROUPS)

    q = gq.reshape(b, L, GLA_HEADS, GLA_DK)
    k = gk.reshape(b, L, GLA_HEADS, GLA_DK)
    v = gv.reshape(b, L, GLA_HEADS, GLA_DV)
    glog = jax.nn.log_sigmoid((glr @ lp['gla_gk_w2'] + lp['gla_gk_b']).astype(F32)) / GLA_GATE_NORMALIZER
    o, gla_new = _gla_chunked(q, k, v, glog.reshape(b, L, GLA_HEADS, GLA_DK), gla_h)
    o_gla = (_head_rmsnorm(o, lp['gla_norm']) * jax.nn.silu(ggate.reshape(b, L, GLA_HEADS, GLA_DV))).reshape(b, L, GLA_VAL_DIM)

    rf_all = jnp.concatenate([shift_buf, rf], axis=1)
    shift_new = rf_all[:, L:]
    rf = rf + (rf_all[:, :L] - rf) * lp['rwkv_mu']
    r7, k7, v7, wl, al = _split(rf, (RWKV_DIM, RWKV_DIM, RWKV_DIM, RWKV_DECAY_RANK, RWKV_ICLR_RANK))
    w_pre = (lp['rwkv_w0'] + jnp.tanh(wl) @ lp['rwkv_w2']).astype(F32)
    decay = jnp.exp(-jnp.exp(-jax.nn.softplus(-w_pre) - 0.5))
    a = jax.nn.sigmoid((lp['rwkv_a0'] + al @ lp['rwkv_a2']).astype(F32)).astype(dtype)
    hd = lambda t: t.reshape(b, L, RWKV_HEADS, RWKV_HEAD)
    kkf = hd(k7 * lp['rwkv_k_k']).astype(F32)
    kk = (kkf / jnp.maximum(jnp.linalg.norm(kkf, axis=-1, keepdims=True), 1e-12)).astype(dtype)
    k7 = k7 * (1 + (a - 1) * lp['rwkv_k_a'])
    r_h, k_h, v_h, a_h = hd(r7), hd(k7), hd(v7), hd(a)
    o7, rwkv_new = _rwkv7_scan(r_h, hd(decay), k_h, v_h, kk, a_h, rwkv_h)
    bonus = jnp.sum(r_h * k_h * lp['rwkv_r_k'], axis=-1, keepdims=True) * v_h
    o_rwkv = (_head_layernorm(o7, lp['rwkv_ln_w'], lp['rwkv_ln_b']) + bonus.reshape(b, L, RWKV_DIM)) * jax.nn.silu(rgate)

    s = jax.nn.sigmoid((merge.reshape(b, L, N_BRANCHES, D_MODEL) + lp['b_merge']).astype(F32)).astype(dtype)
    m = (s[:, :, 0] * (o_ssd @ lp['w_proj_ssd'])
         + s[:, :, 1] * (o_gla @ lp['w_proj_gla'])
         + s[:, :, 2] * (o_rwkv @ lp['w_proj_rwkv']))
    out = m @ lp['w_out']
    return out, (ssd_new, conv_new, gla_new, rwkv_new, shift_new)


def _cross_attn(u, mk, mv, wq, wo):
    b, L, _ = u.shape
    q = (u @ wq).reshape(b, L, XA_HEADS, XA_HEAD_DIM)
    sc = jnp.einsum('blhd,bmhd->bhlm', q, mk).astype(F32) * (XA_HEAD_DIM ** -0.5)
    p = jax.nn.softmax(sc, axis=-1).astype(u.dtype)
    o = jnp.einsum('bhlm,bmhd->blhd', p, mv).reshape(b, L, D_MODEL)
    return o @ wo


def _mem_kv(mem, g, wk, wv):
    b, M, _ = mem.shape
    mn = _rmsnorm(mem, g)
    return ((mn @ wk).reshape(b, M, XA_HEADS, XA_HEAD_DIM), (mn @ wv).reshape(b, M, XA_HEADS, XA_HEAD_DIM))


def _trunk(h, mem_k, mem_v, ssd_h, conv_buf, gla_h, rwkv_h, shift_buf, P):
    new = ([], [], [], [], [])
    for l in range(DEPTH):
        lp = {name: arr[l] for name, arr in P.items()}
        mix, st = _mixer(_rmsnorm(h, lp['norm_mix']), lp, ssd_h[l], conv_buf[l], gla_h[l], rwkv_h[l], shift_buf[l])
        h = h + mix
        h = h + _cross_attn(_rmsnorm(h, lp['norm_xattn']), mem_k[l], mem_v[l], lp['xa_wq'], lp['xa_wo'])
        for lst, s_ in zip(new, st):
            lst.append(s_)
    return h, tuple(jnp.stack(lst, axis=0) for lst in new)


def setup_inputs(seed: int = 0) -> dict:
    key = jax.random.key(seed)
    ks = iter(jax.random.split(key, 64))

    def nrm(shape, scale):
        return scale * jax.random.normal(next(ks), shape, F32)

    def gain(shape):
        return 1.0 + nrm(shape, 0.02)

    Dm = D_MODEL
    x_prompt = nrm((BATCH, SEQ, Dm), 1.0)
    x_sample = nrm((DEC_BATCH, DEC_SEQ, Dm), 1.0)
    mem_prompt = nrm((BATCH, MEM_LEN, Dm), 1.0)
    state_ssd = nrm((DEPTH, DEC_BATCH, SSD_HEADS, SSD_HEAD_DIM, SSD_STATE), 0.1)
    state_ssd_conv = nrm((DEPTH, DEC_BATCH, SSD_CONV - 1, SSD_CONV_DIM), 1.0)
    state_gla = nrm((DEPTH, DEC_BATCH, GLA_HEADS, GLA_DK, GLA_DV), 0.1)
    state_rwkv = nrm((DEPTH, DEC_BATCH, RWKV_HEADS, RWKV_HEAD, RWKV_HEAD), 0.3)
    state_rwkv_shift = nrm((DEPTH, DEC_BATCH, 1, RWKV_SHIFT_COLS), 1.0)
    cache_mem_k = nrm((DEPTH, DEC_BATCH, MEM_LEN, XA_HEADS, XA_HEAD_DIM), 1.0)
    cache_mem_v = nrm((DEPTH, DEC_BATCH, MEM_LEN, XA_HEADS, XA_HEAD_DIM), 1.0)
    norm_mix = gain((DEPTH, Dm))
    w_in = nrm((DEPTH, Dm, IN_COLS), Dm ** -0.5)
    ssd_conv_w = nrm((DEPTH, SSD_CONV, SSD_CONV_DIM), 0.5)
    ssd_conv_b = nrm((DEPTH, SSD_CONV_DIM), 0.02)
    dt0 = jnp.exp(jax.random.uniform(next(ks), (DEPTH, SSD_HEADS), F32, math.log(1e-3), math.log(1e-1)))
    ssd_dt_bias = dt0 + jnp.log(-jnp.expm1(-dt0))
    ssd_A_log = jnp.log(jax.random.uniform(next(ks), (DEPTH, SSD_HEADS), F32, 1.0, 16.0))
    ssd_D = 1.0 + nrm((DEPTH, SSD_HEADS), 0.1)
    ssd_norm = gain((DEPTH, SSD_INNER))
    w_proj_ssd = nrm((DEPTH, SSD_INNER, Dm), SSD_INNER ** -0.5)
    gla_gk_w2 = nrm((DEPTH, GLA_GATE_RANK, GLA_KEY_DIM), GLA_GATE_RANK ** -0.5)
    gla_gk_b = nrm((DEPTH, GLA_KEY_DIM), 0.1)
    gla_norm = gain((DEPTH, GLA_DV))
    w_proj_gla = nrm((DEPTH, GLA_VAL_DIM, Dm), GLA_VAL_DIM ** -0.5)
    rwkv_mu = jax.random.uniform(next(ks), (DEPTH, RWKV_SHIFT_COLS), F32)
    rwkv_w0 = jax.random.uniform(next(ks), (DEPTH, RWKV_DIM), F32, -6.0, 1.0)
    rwkv_w2 = nrm((DEPTH, RWKV_DECAY_RANK, RWKV_DIM), 0.1)
    rwkv_a0 = nrm((DEPTH, RWKV_DIM), 0.1)
    rwkv_a2 = nrm((DEPTH, RWKV_ICLR_RANK, RWKV_DIM), 0.1)
    rwkv_k_k = 0.85 + nrm((DEPTH, RWKV_DIM), 0.05)
    rwkv_k_a = 1.0 + nrm((DEPTH, RWKV_DIM), 0.05)
    rwkv_r_k = nrm((DEPTH, RWKV_HEADS, RWKV_HEAD), 0.1)
    rwkv_ln_w = gain((DEPTH, RWKV_DIM))
    rwkv_ln_b = nrm((DEPTH, RWKV_DIM), 0.02)
    w_proj_rwkv = nrm((DEPTH, RWKV_DIM, Dm), RWKV_DIM ** -0.5)
    b_merge = nrm((DEPTH, N_BRANCHES, Dm), 0.1)
    w_out = nrm((DEPTH, Dm, Dm), Dm ** -0.5)
    norm_xattn = gain((DEPTH, Dm))
    xa_wq = nrm((DEPTH, Dm, Dm), Dm ** -0.5)
    xa_wo = nrm((DEPTH, Dm, Dm), Dm ** -0.5)
    norm_mem = gain((DEPTH, Dm))
    xa_wk = nrm((DEPTH, Dm, Dm), Dm ** -0.5)
    xa_wv = nrm((DEPTH, Dm, Dm), Dm ** -0.5)
    norm_final = gain((Dm,))
    return {'x_prompt': x_prompt, 'x_sample': x_sample, 'mem_prompt': mem_prompt,
            'state_ssd': state_ssd, 'state_ssd_conv': state_ssd_conv, 'state_gla': state_gla,
            'state_rwkv': state_rwkv, 'state_rwkv_shift': state_rwkv_shift,
            'cache_mem_k': cache_mem_k, 'cache_mem_v': cache_mem_v,
            'norm_mix': norm_mix, 'w_in': w_in, 'ssd_conv_w': ssd_conv_w, 'ssd_conv_b': ssd_conv_b,
            'ssd_dt_bias': ssd_dt_bias, 'ssd_A_log': ssd_A_log, 'ssd_D': ssd_D, 'ssd_norm': ssd_norm,
            'w_proj_ssd': w_proj_ssd, 'gla_gk_w2': gla_gk_w2, 'gla_gk_b': gla_gk_b, 'gla_norm': gla_norm,
            'w_proj_gla': w_proj_gla, 'rwkv_mu': rwkv_mu, 'rwkv_w0': rwkv_w0, 'rwkv_w2': rwkv_w2,
            'rwkv_a0': rwkv_a0, 'rwkv_a2': rwkv_a2, 'rwkv_k_k': rwkv_k_k, 'rwkv_k_a': rwkv_k_a,
            'rwkv_r_k': rwkv_r_k, 'rwkv_ln_w': rwkv_ln_w, 'rwkv_ln_b': rwkv_ln_b, 'w_proj_rwkv': w_proj_rwkv,
            'b_merge': b_merge, 'w_out': w_out, 'norm_xattn': norm_xattn, 'xa_wq': xa_wq, 'xa_wo': xa_wo,
            'norm_mem': norm_mem, 'xa_wk': xa_wk, 'xa_wv': xa_wv, 'norm_final': norm_final}


def reference(x_prompt, x_sample, mem_prompt, state_ssd, state_ssd_conv, state_gla, state_rwkv, state_rwkv_shift,
              cache_mem_k, cache_mem_v, norm_mix, w_in, ssd_conv_w, ssd_conv_b, ssd_dt_bias, ssd_A_log, ssd_D,
              ssd_norm, w_proj_ssd, gla_gk_w2, gla_gk_b, gla_norm, w_proj_gla, rwkv_mu, rwkv_w0, rwkv_w2,
              rwkv_a0, rwkv_a2, rwkv_k_k, rwkv_k_a, rwkv_r_k, rwkv_ln_w, rwkv_ln_b, w_proj_rwkv, b_merge, w_out,
              norm_xattn, xa_wq, xa_wo, norm_mem, xa_wk, xa_wv, norm_final):
    P = dict(norm_mix=norm_mix, w_in=w_in, ssd_conv_w=ssd_conv_w, ssd_conv_b=ssd_conv_b, ssd_dt_bias=ssd_dt_bias,
             ssd_A_log=ssd_A_log, ssd_D=ssd_D, ssd_norm=ssd_norm, w_proj_ssd=w_proj_ssd, gla_gk_w2=gla_gk_w2,
             gla_gk_b=gla_gk_b, gla_norm=gla_norm, w_proj_gla=w_proj_gla, rwkv_mu=rwkv_mu, rwkv_w0=rwkv_w0,
             rwkv_w2=rwkv_w2, rwkv_a0=rwkv_a0, rwkv_a2=rwkv_a2, rwkv_k_k=rwkv_k_k, rwkv_k_a=rwkv_k_a,
             rwkv_r_k=rwkv_r_k, rwkv_ln_w=rwkv_ln_w, rwkv_ln_b=rwkv_ln_b, w_proj_rwkv=w_proj_rwkv,
             b_merge=b_merge, w_out=w_out, norm_xattn=norm_xattn, xa_wq=xa_wq, xa_wo=xa_wo)

    kv = [_mem_kv(mem_prompt, norm_mem[l], xa_wk[l], xa_wv[l]) for l in range(DEPTH)]
    mem_k_p = jnp.stack([t[0] for t in kv], axis=0)
    mem_v_p = jnp.stack([t[1] for t in kv], axis=0)
    bp = x_prompt.shape[0]
    zeros = lambda shape: jnp.zeros((DEPTH, bp) + shape, x_prompt.dtype)
    hp, (p_ssd, p_conv, p_gla, p_rwkv, p_shift) = _trunk(
        x_prompt, mem_k_p, mem_v_p,
        zeros((SSD_HEADS, SSD_HEAD_DIM, SSD_STATE)), zeros((SSD_CONV - 1, SSD_CONV_DIM)),
        zeros((GLA_HEADS, GLA_DK, GLA_DV)), zeros((RWKV_HEADS, RWKV_HEAD, RWKV_HEAD)),
        zeros((1, RWKV_SHIFT_COLS)), P)
    y_prompt = _rmsnorm(hp, norm_final)

    hs, (s_ssd, s_conv, s_gla, s_rwkv, s_shift) = _trunk(
        x_sample, cache_mem_k, cache_mem_v, state_ssd, state_ssd_conv, state_gla, state_rwkv, state_rwkv_shift, P)
    y_sample = _rmsnorm(hs, norm_final)

    return (y_prompt, y_sample, p_ssd, p_conv, p_gla, p_rwkv, p_shift, mem_k_p, mem_v_p,
            s_ssd, s_conv, s_gla, s_rwkv, s_shift)
```

```python
import functools

import jax
import jax.numpy as jnp
from jax import lax
from jax.experimental import pallas as pl
from jax.experimental.pallas import tpu as pltpu

F32 = jnp.float32
BF16 = jnp.bfloat16

D_MODEL = 1024
NORM_EPS = 1e-5

SSD_HEADS = 16
SSD_HEAD_DIM = 64
SSD_GROUPS = 2
SSD_STATE = 128
SSD_CONV = 4
SSD_CONV_DIM = D_MODEL + 2 * SSD_GROUPS * SSD_STATE
SSD_GROUP_W = D_MODEL // SSD_GROUPS

GLA_HEADS = 4
GLA_DK = 128
GLA_DV = 256
GLA_KEY_DIM = GLA_HEADS * GLA_DK
GLA_GATE_RANK = 16
GLA_GATE_NORMALIZER = 16.0
GLA_CHUNK = 64

RWKV_HEADS = 16
RWKV_HEAD = 64
RWKV_LORA = 64
RWKV_SHIFT_COLS = 3 * D_MODEL + 2 * RWKV_LORA
RWKV_LN_EPS = 64e-5
RWKV_CHUNK = 64

XA_HEADS = 4
XA_HEAD_DIM = 256
N_BRANCHES = 3

LANE = 128
SEG_TILE = 256
VMEM_LIMIT = 56 * 1024 * 1024

_IN_SPLITS = (D_MODEL, SSD_CONV_DIM, SSD_HEADS, GLA_KEY_DIM, GLA_KEY_DIM, D_MODEL, D_MODEL,
              GLA_GATE_RANK, RWKV_SHIFT_COLS, D_MODEL, N_BRANCHES * D_MODEL)


def _mm(a, b):
    return jnp.dot(a.astype(BF16), b.astype(BF16), preferred_element_type=F32)


def _mm_nt(a, b):
    return lax.dot_general(a.astype(BF16), b.astype(BF16), (((1,), (1,)), ((), ())),
                           preferred_element_type=F32)


def _mm_tn(a, b):
    return lax.dot_general(a.astype(BF16), b.astype(BF16), (((0,), (0,)), ((), ())),
                           preferred_element_type=F32)


def _split3(x):
    hi = x.astype(BF16)
    r1 = x - hi.astype(F32)
    mid = r1.astype(BF16)
    lo = (r1 - mid.astype(F32)).astype(BF16)
    return hi, mid, lo


def _mm_exact_lhs(m_bf16, x):
    hi, mid, lo = _split3(x)
    dot = functools.partial(jnp.dot, preferred_element_type=F32)
    return dot(m_bf16, hi) + dot(m_bf16, mid) + dot(m_bf16, lo)


def _mm_exact_rhs(x, m_bf16):
    hi, mid, lo = _split3(x)
    dot = functools.partial(jnp.dot, preferred_element_type=F32)
    return dot(hi, m_bf16) + dot(mid, m_bf16) + dot(lo, m_bf16)


def _lower_tri(n, strict=False):
    r = lax.broadcasted_iota(jnp.int32, (n, n), 0)
    c = lax.broadcasted_iota(jnp.int32, (n, n), 1)
    return (r > c) if strict else (r >= c)


def _seg_ones(width, seg):
    r = lax.broadcasted_iota(jnp.int32, (width, width), 0) // seg
    c = lax.broadcasted_iota(jnp.int32, (width, width), 1) // seg
    return jnp.where(r == c, 1.0, 0.0).astype(BF16)


def _head_sums(x, seg):
    ones = _seg_ones(SEG_TILE, seg)
    parts = [_mm_exact_rhs(x[:, j:j + SEG_TILE], ones) for j in range(0, x.shape[1], SEG_TILE)]
    return jnp.concatenate(parts, axis=1)


def _rms(x, g):
    return x * lax.rsqrt(jnp.mean(x * x, axis=-1, keepdims=True) + NORM_EPS) * g


def _softplus(x):
    return jnp.maximum(x, 0.0) + jnp.log1p(jnp.exp(-jnp.abs(x)))


def _sigmoid(x):
    return 1.0 / (1.0 + jnp.exp(-x))


def _silu(x):
    return x * _sigmoid(x)


def _ssd_kernel(h_ref, g_ref, w_ref, convw_ref, convb_ref, dtbx_ref, alogx_ref, dtbc_ref, alogc_ref,
                dx_ref, gn_ref, st0_ref, conv0_ref,
                o_ref, st_out_ref, conv_out_ref,
                st_scr, xbuf_scr, *, tc):
    c = pl.program_id(1)
    nc = pl.num_programs(1)
    C = SSD_CONV_DIM
    W = D_MODEL

    @pl.when(c == 0)
    def _init():
        st_scr[...] = st0_ref[0].reshape(W, SSD_STATE).T
        xbuf_scr[0:8, :] = jnp.zeros((8, C), F32)
        xbuf_scr[5:8, :] = conv0_ref[0]

    u = _rms(h_ref[0], g_ref[...]).astype(BF16)
    z = jnp.dot(u, w_ref[:, 0:W], preferred_element_type=F32)
    xbuf_scr[8:8 + tc, :] = jnp.dot(u, w_ref[:, W:W + C], preferred_element_type=F32)
    dtx_raw = jnp.dot(u, w_ref[:, W + C:2 * W + C], preferred_element_type=F32)
    dtc_raw = jnp.dot(u, w_ref[:, 2 * W + C:2 * W + C + LANE], preferred_element_type=F32)

    conv = convb_ref[...] + convw_ref[SSD_CONV - 1:SSD_CONV, :] * xbuf_scr[8:8 + tc, :]
    for j in range(SSD_CONV - 1):
        conv = conv + convw_ref[j:j + 1, :] * xbuf_scr[5 + j:5 + j + tc, :]
    conv_out_ref[0] = xbuf_scr[tc + 5:tc + 8, :]
    xbuf_scr[0:8, :] = xbuf_scr[tc:tc + 8, :]

    xbc = _silu(conv)
    xs = xbc[:, 0:W]
    bm = xbc[:, W:W + SSD_GROUPS * SSD_STATE].astype(BF16)
    cm = xbc[:, W + SSD_GROUPS * SSD_STATE:C].astype(BF16)

    dtx = _softplus(dtx_raw + dtbx_ref[...])
    dtc = _softplus(dtc_raw + dtbc_ref[...])
    tri = jnp.where(_lower_tri(tc), 1.0, 0.0).astype(BF16)
    acs_x = _mm_exact_lhs(tri, dtx * -jnp.exp(alogx_ref[...]))
    acs_c = _mm_exact_lhs(tri, dtc * -jnp.exp(alogc_ref[...]))
    acs_ct = acs_c.T
    last = acs_x[tc - 1:tc, :]
    e_acs = jnp.exp(acs_x)
    xd = xs * dtx
    xd_b = xd.astype(BF16)
    xd_end = (xd * jnp.exp(last - acs_x)).astype(BF16)
    chunk_decay = jnp.exp(last)
    causal = _lower_tri(tc)

    ys = []
    for g in range(SSD_GROUPS):
        gs = slice(g * SSD_GROUP_W, (g + 1) * SSD_GROUP_W)
        bg = bm[:, g * SSD_STATE:(g + 1) * SSD_STATE]
        cg = cm[:, g * SSD_STATE:(g + 1) * SSD_STATE]
        cb = _mm_nt(cg, bg)
        st_g = st_scr[:, gs]
        y_off = _mm(cg, st_g) * e_acs[:, gs]
        parts = []
        for e in range(SSD_HEADS // SSD_GROUPS):
            hh = g * (SSD_HEADS // SSD_GROUPS) + e
            seg = acs_c[:, hh:hh + 1] - acs_ct[hh:hh + 1, :]
            lmat = jnp.exp(jnp.where(causal, seg, -jnp.inf))
            parts.append(_mm(cb * lmat, xd_b[:, hh * SSD_HEAD_DIM:(hh + 1) * SSD_HEAD_DIM]))
        ys.append(jnp.concatenate(parts, axis=1) + y_off)
        st_scr[:, gs] = st_g * chunk_decay[:, gs] + _mm_tn(bg, xd_end[:, gs])
    y = jnp.concatenate(ys, axis=1) + xs * dx_ref[...]

    t = y * _silu(z)
    outs = []
    for g in range(SSD_GROUPS):
        tg = t[:, g * SSD_GROUP_W:(g + 1) * SSD_GROUP_W]
        outs.append(tg * lax.rsqrt(jnp.mean(tg * tg, axis=-1, keepdims=True) + NORM_EPS))
    o_ref[0] = (jnp.concatenate(outs, axis=1) * gn_ref[...]).astype(o_ref.dtype)

    @pl.when(c == nc - 1)
    def _fin():
        st_out_ref[0] = st_scr[...].T.reshape(SSD_HEADS, SSD_HEAD_DIM, SSD_STATE)


def _const_spec(shape):
    nd = len(shape)
    return pl.BlockSpec(shape, lambda b, c: (0,) * nd)


def _ssd_call(h, lw, st0, conv0, tc):
    B, L, _ = h.shape
    nc = L // tc
    consts = [lw['norm_mix'], lw['w_ssd'], lw['ssd_conv_w'], lw['ssd_conv_b'], lw['ssd_dtb_x'], lw['ssd_alog_x'],
              lw['ssd_dtb_c'], lw['ssd_alog_c'], lw['ssd_d_x'], lw['ssd_norm']]
    in_specs = ([pl.BlockSpec((1, tc, D_MODEL), lambda b, c: (b, c, 0))]
                + [_const_spec(a.shape) for a in consts]
                + [pl.BlockSpec((1, SSD_HEADS, SSD_HEAD_DIM, SSD_STATE), lambda b, c: (b, 0, 0, 0)),
                   pl.BlockSpec((1, SSD_CONV - 1, SSD_CONV_DIM), lambda b, c: (b, 0, 0))])
    out_specs = [pl.BlockSpec((1, tc, D_MODEL), lambda b, c: (b, c, 0)),
                 pl.BlockSpec((1, SSD_HEADS, SSD_HEAD_DIM, SSD_STATE), lambda b, c: (b, 0, 0, 0)),
                 pl.BlockSpec((1, SSD_CONV - 1, SSD_CONV_DIM), lambda b, c: (b, 0, 0))]
    out_shape = [jax.ShapeDtypeStruct((B, L, D_MODEL), BF16),
                 jax.ShapeDtypeStruct((B, SSD_HEADS, SSD_HEAD_DIM, SSD_STATE), F32),
                 jax.ShapeDtypeStruct((B, SSD_CONV - 1, SSD_CONV_DIM), F32)]
    return pl.pallas_call(
        functools.partial(_ssd_kernel, tc=tc),
        grid=(B, nc), in_specs=in_specs, out_specs=out_specs, out_shape=out_shape,
        scratch_shapes=[pltpu.VMEM((SSD_STATE, D_MODEL), F32), pltpu.VMEM((tc + 8, SSD_CONV_DIM), F32)],
        compiler_params=pltpu.CompilerParams(dimension_semantics=("arbitrary", "arbitrary"),
                                             vmem_limit_bytes=VMEM_LIMIT),
        name="ssd_mixer",
    )(h, *consts, st0, conv0)


def _gla_kernel(h_ref, g_ref, w_ref, w2_ref, gb_ref, gn_ref, s0_ref,
                o_ref, s_out_ref,
                s_scr, q_scr, k_scr, v_scr, gl_scr, o_scr, *, tc, q):
    c = pl.program_id(1)
    nc = pl.num_programs(1)
    KD = GLA_KEY_DIM
    W = D_MODEL

    @pl.when(c == 0)
    def _init():
        for hd in range(GLA_HEADS):
            s_scr[hd] = s0_ref[0, hd].T

    u = _rms(h_ref[0], g_ref[...]).astype(BF16)
    q_scr[...] = jnp.dot(u, w_ref[:, 0:KD], preferred_element_type=F32) * (GLA_DK ** -0.5)
    k_scr[...] = jnp.dot(u, w_ref[:, KD:2 * KD], preferred_element_type=F32)
    v_scr[...] = jnp.dot(u, w_ref[:, 2 * KD:2 * KD + W], preferred_element_type=F32)
    glr = jnp.dot(u, w_ref[:, 2 * KD + 2 * W:2 * KD + 2 * W + LANE], preferred_element_type=F32)
    gpre = _mm(glr, w2_ref[...]) + gb_ref[...]
    gl_scr[...] = -_softplus(-gpre) / GLA_GATE_NORMALIZER

    tri = jnp.where(_lower_tri(q), 1.0, 0.0).astype(BF16)
    causal = _lower_tri(q)

    def chunk(j, carry):
        rows = pl.ds(pl.multiple_of(j * q, q), q)
        gcs = _mm_exact_lhs(tri, gl_scr[rows, :])
        last = gcs[q - 1:q, :]
        kk = k_scr[rows, :]
        qe = (q_scr[rows, :] * jnp.exp(gcs)).astype(BF16)
        ke = (kk * jnp.exp(-gcs)).astype(BF16)
        kend = (kk * jnp.exp(last - gcs)).astype(BF16)
        dec = jnp.exp(last)
        vv = v_scr[rows, :].astype(BF16)
        for hd in range(GLA_HEADS):
            ks = slice(hd * GLA_DK, (hd + 1) * GLA_DK)
            vs = slice(hd * GLA_DV, (hd + 1) * GLA_DV)
            a = jnp.where(causal, _mm_nt(qe[:, ks], ke[:, ks]), 0.0)
            s_t = s_scr[hd]
            o_scr[rows, vs] = _mm(a, vv[:, vs]) + _mm_nt(qe[:, ks], s_t)
            s_scr[hd] = s_t * dec[:, ks] + _mm_tn(vv[:, vs], kend[:, ks])
        return carry

    lax.fori_loop(0, tc // q, chunk, 0)

    gate = _silu(jnp.dot(u, w_ref[:, 2 * KD + W:2 * KD + 2 * W], preferred_element_type=F32))
    o = o_scr[...]
    outs = []
    for hd in range(GLA_HEADS):
        oh = o[:, hd * GLA_DV:(hd + 1) * GLA_DV]
        outs.append(oh * lax.rsqrt(jnp.mean(oh * oh, axis=-1, keepdims=True) + NORM_EPS))
    o_ref[0] = (jnp.concatenate(outs, axis=1) * gn_ref[...] * gate).astype(o_ref.dtype)

    @pl.when(c == nc - 1)
    def _fin():
        for hd in range(GLA_HEADS):
            s_out_ref[0, hd] = s_scr[hd].T


def _gla_call(h, lw, s0, tc, q):
    B, L, _ = h.shape
    nc = L // tc
    consts = [lw['norm_mix'], lw['w_gla'], lw['gla_w2'], lw['gla_gb'], lw['gla_norm']]
    in_specs = ([pl.BlockSpec((1, tc, D_MODEL), lambda b, c: (b, c, 0))]
                + [_const_spec(a.shape) for a in consts]
                + [pl.BlockSpec((1, GLA_HEADS, GLA_DK, GLA_DV), lambda b, c: (b, 0, 0, 0))])
    out_specs = [pl.BlockSpec((1, tc, D_MODEL), lambda b, c: (b, c, 0)),
                 pl.BlockSpec((1, GLA_HEADS, GLA_DK, GLA_DV), lambda b, c: (b, 0, 0, 0))]
    out_shape = [jax.ShapeDtypeStruct((B, L, D_MODEL), BF16),
                 jax.ShapeDtypeStruct((B, GLA_HEADS, GLA_DK, GLA_DV), F32)]
    return pl.pallas_call(
        functools.partial(_gla_kernel, tc=tc, q=q),
        grid=(B, nc), in_specs=in_specs, out_specs=out_specs, out_shape=out_shape,
        scratch_shapes=[pltpu.VMEM((GLA_HEADS, GLA_DV, GLA_DK), F32),
                        pltpu.VMEM((tc, GLA_KEY_DIM), F32), pltpu.VMEM((tc, GLA_KEY_DIM), F32),
                        pltpu.VMEM((tc, D_MODEL), F32), pltpu.VMEM((tc, GLA_KEY_DIM), F32),
                        pltpu.VMEM((tc, D_MODEL), F32)],
        compiler_params=pltpu.CompilerParams(dimension_semantics=("arbitrary", "arbitrary"),
                                             vmem_limit_bytes=VMEM_LIMIT),
        name="gla_mixer",
    )(h, *consts, s0)


def _solve_unit_lower(n, z, steps):
    x = z
    p = n
    for s in range(steps):
        x = x + _mm(p, x)
        if s + 1 < steps:
            p = _mm(p, p)
    return x


def _rwkv_kernel(h_ref, g_ref, w_ref, mu_ref, w0_ref, w2_ref, a0_ref, a2_ref, kk_ref, ka_ref, rk_ref,
                 lnw_ref, lnb_ref, s0_ref, sh0_ref,
                 o_ref, s_out_ref, sh_out_ref,
                 s_scr, carry_scr, r_scr, k_scr, v_scr, p_scr, q_scr, lw_scr, o_scr, *, tc, q):
    c = pl.program_id(1)
    nc = pl.num_programs(1)
    W = D_MODEL
    N = RWKV_HEAD

    @pl.when(c == 0)
    def _init():
        s_scr[...] = s0_ref[0]
        carry_scr[...] = sh0_ref[0]

    u = _rms(h_ref[0], g_ref[...]).astype(BF16)
    rf = jnp.dot(u, w_ref[:, 0:RWKV_SHIFT_COLS], preferred_element_type=F32)
    first = lax.broadcasted_iota(jnp.int32, (tc, 1), 0) == 0
    prev = jnp.where(first, carry_scr[...], pltpu.roll(rf, 1, axis=0))
    carry_scr[...] = rf[tc - 1:tc, :]
    sh_out_ref[0] = rf[tc - 1:tc, :]
    rfm = rf + (prev - rf) * mu_ref[...]
    r7 = rfm[:, 0:W]
    k7 = rfm[:, W:2 * W]
    v7 = rfm[:, 2 * W:3 * W]
    lora = rfm[:, 3 * W:3 * W + 2 * RWKV_LORA]
    w_pre = w0_ref[...] + _mm(jnp.tanh(lora), w2_ref[...])
    lw_scr[...] = -jnp.exp(-_softplus(-w_pre) - 0.5)
    a = _sigmoid(a0_ref[...] + _mm(lora, a2_ref[...]))
    kkf = k7 * kk_ref[...]
    kk = kkf / jnp.maximum(jnp.sqrt(_head_sums(kkf * kkf, N)), 1e-12)
    k7 = k7 * (1.0 + (a - 1.0) * ka_ref[...])
    bonus = _head_sums(r7 * k7 * rk_ref[...], N) * v7
    r_scr[...] = r7
    k_scr[...] = k7
    v_scr[...] = v7
    p_scr[...] = -kk
    q_scr[...] = kk * a

    tri = jnp.where(_lower_tri(q), 1.0, 0.0).astype(BF16)
    incl = _lower_tri(q)
    strict = _lower_tri(q, strict=True)
    steps = max(1, (q - 1).bit_length())

    def chunk(j, carry):
        rows = pl.ds(pl.multiple_of(j * q, q), q)
        lw = lw_scr[rows, :]
        cum = _mm_exact_lhs(tri, lw)
        last = cum[q - 1:q, :]
        e_in = jnp.exp(cum)
        e_inv = jnp.exp(-cum)
        e_end = jnp.exp(last - cum)
        k_all = k_scr[rows, :]
        q_all = q_scr[rows, :]
        rt = (r_scr[rows, :] * e_in).astype(BF16)
        pt = (p_scr[rows, :] * jnp.exp(cum - lw)).astype(BF16)
        qt = (q_all * e_inv).astype(BF16)
        kt = (k_all * e_inv).astype(BF16)
        qend = (q_all * e_end).astype(BF16)
        kend = (k_all * e_end).astype(BF16)
        dec = jnp.exp(last)
        vv = v_scr[rows, :].astype(BF16)
        for hd in range(RWKV_HEADS):
            hs = slice(hd * N, (hd + 1) * N)
            s0 = s_scr[hd]
            a_pq = jnp.where(strict, _mm_nt(pt[:, hs], qt[:, hs]), 0.0)
            a_pk = jnp.where(strict, _mm_nt(pt[:, hs], kt[:, hs]), 0.0)
            a_rq = jnp.where(incl, _mm_nt(rt[:, hs], qt[:, hs]), 0.0)
            a_rk = jnp.where(incl, _mm_nt(rt[:, hs], kt[:, hs]), 0.0)
            vh = vv[:, hs]
            uu = _solve_unit_lower(a_pq, _mm_nt(pt[:, hs], s0) + _mm(a_pk, vh), steps)
            o_scr[rows, hs] = _mm_nt(rt[:, hs], s0) + _mm(a_rq, uu) + _mm(a_rk, vh)
            s_scr[hd] = s0 * dec[:, hs] + _mm_tn(uu, qend[:, hs]) + _mm_tn(vh, kend[:, hs])
        return carry

    lax.fori_loop(0, tc // q, chunk, 0)

    o7 = o_scr[...]
    mean = _head_sums(o7, N) * (1.0 / N)
    d = o7 - mean
    var = _head_sums(d * d, N) * (1.0 / N)
    ln = d * lax.rsqrt(var + RWKV_LN_EPS) * lnw_ref[...] + lnb_ref[...]
    gate = _silu(jnp.dot(u, w_ref[:, RWKV_SHIFT_COLS:RWKV_SHIFT_COLS + W], preferred_element_type=F32))
    o_ref[0] = ((ln + bonus) * gate).astype(o_ref.dtype)

    @pl.when(c == nc - 1)
    def _fin():
        s_out_ref[0] = s_scr[...]


def _rwkv_call(h, lw, s0, sh0, tc, q):
    B, L, _ = h.shape
    nc = L // tc
    consts = [lw['norm_mix'], lw['w_rwkv'], lw['rwkv_mu'], lw['rwkv_w0'], lw['rwkv_w2'], lw['rwkv_a0'], lw['rwkv_a2'],
              lw['rwkv_k_k'], lw['rwkv_k_a'], lw['rwkv_r_k'], lw['rwkv_ln_w'], lw['rwkv_ln_b']]
    in_specs = ([pl.BlockSpec((1, tc, D_MODEL), lambda b, c: (b, c, 0))]
                + [_const_spec(a.shape) for a in consts]
                + [pl.BlockSpec((1, RWKV_HEADS, RWKV_HEAD, RWKV_HEAD), lambda b, c: (b, 0, 0, 0)),
                   pl.BlockSpec((1, 1, RWKV_SHIFT_COLS), lambda b, c: (b, 0, 0))])
    out_specs = [pl.BlockSpec((1, tc, D_MODEL), lambda b, c: (b, c, 0)),
                 pl.BlockSpec((1, RWKV_HEADS, RWKV_HEAD, RWKV_HEAD), lambda b, c: (b, 0, 0, 0)),
                 pl.BlockSpec((1, 1, RWKV_SHIFT_COLS), lambda b, c: (b, 0, 0))]
    out_shape = [jax.ShapeDtypeStruct((B, L, D_MODEL), BF16),
                 jax.ShapeDtypeStruct((B, RWKV_HEADS, RWKV_HEAD, RWKV_HEAD), F32),
                 jax.ShapeDtypeStruct((B, 1, RWKV_SHIFT_COLS), F32)]
    big = pltpu.VMEM((tc, D_MODEL), F32)
    return pl.pallas_call(
        functools.partial(_rwkv_kernel, tc=tc, q=q),
        grid=(B, nc), in_specs=in_specs, out_specs=out_specs, out_shape=out_shape,
        scratch_shapes=[pltpu.VMEM((RWKV_HEADS, RWKV_HEAD, RWKV_HEAD), F32),
                        pltpu.VMEM((1, RWKV_SHIFT_COLS), F32),
                        big, big, big, big, big, big, big],
        compiler_params=pltpu.CompilerParams(dimension_semantics=("arbitrary", "arbitrary"),
                                             vmem_limit_bytes=VMEM_LIMIT),
        name="rwkv_mixer",
    )(h, *consts, s0, sh0)


def _merge_kernel(h_ref, os_ref, og_ref, or_ref, mk_ref, mv_ref,
                  g_ref, wm_ref, bm_ref, ps_ref, pg_ref, pr_ref, wo_ref, gx_ref, xq_ref, xo_ref, gf_ref,
                  h_out_ref, y_ref):
    W = D_MODEL
    h = h_ref[0]
    u = _rms(h, g_ref[...]).astype(BF16)
    m = None
    for i, (branch_ref, proj_ref) in enumerate(((os_ref, ps_ref), (og_ref, pg_ref), (or_ref, pr_ref))):
        s = _sigmoid(jnp.dot(u, wm_ref[:, i * W:(i + 1) * W], preferred_element_type=F32)
                     + bm_ref[:, i * W:(i + 1) * W])
        term = s * jnp.dot(branch_ref[0], proj_ref[...], preferred_element_type=F32)
        m = term if m is None else m + term
    h1 = h + _mm(m, wo_ref[...])

    u2 = _rms(h1, gx_ref[...])
    qx = _mm(u2, xq_ref[...])
    outs = []
    for hd in range(XA_HEADS):
        hs = slice(hd * XA_HEAD_DIM, (hd + 1) * XA_HEAD_DIM)
        sc = _mm_nt(qx[:, hs], mk_ref[0, :, hs]) * (XA_HEAD_DIM ** -0.5)
        sc = sc - jnp.max(sc, axis=-1, keepdims=True)
        e = jnp.exp(sc)
        p = e / jnp.sum(e, axis=-1, keepdims=True)
        outs.append(_mm(p, mv_ref[0, :, hs]))
    h2 = h1 + _mm(jnp.concatenate(outs, axis=1), xo_ref[...])
    h_out_ref[0] = h2
    y_ref[0] = _rms(h2, gf_ref[...])


def _merge_call(h, o_ssd, o_gla, o_rwkv, mem_k, mem_v, lw, norm_final, tm):
    B, L, _ = h.shape
    M = mem_k.shape[1]
    consts = [lw['norm_mix'], lw['w_merge'], lw['b_merge'], lw['w_proj_ssd'], lw['w_proj_gla'], lw['w_proj_rwkv'],
              lw['w_out'], lw['norm_xattn'], lw['xa_wq'], lw['xa_wo'], norm_final]
    tok = pl.BlockSpec((1, tm, D_MODEL), lambda b, c: (b, c, 0))
    mem = pl.BlockSpec((1, M, D_MODEL), lambda b, c: (b, 0, 0))
    return pl.pallas_call(
        _merge_kernel,
        grid=(B, L // tm),
        in_specs=[tok, tok, tok, tok, mem, mem] + [_const_spec(a.shape) for a in consts],
        out_specs=[tok, tok],
        out_shape=[jax.ShapeDtypeStruct((B, L, D_MODEL), F32), jax.ShapeDtypeStruct((B, L, D_MODEL), F32)],
        compiler_params=pltpu.CompilerParams(dimension_semantics=("arbitrary", "arbitrary"),
                                             vmem_limit_bytes=VMEM_LIMIT),
        name="merge_xattn",
    )(h, o_ssd, o_gla, o_rwkv, mem_k, mem_v, *consts)


def _memkv_kernel(x_ref, g_ref, wk_ref, wv_ref, k_ref, v_ref):
    u = _rms(x_ref[...], g_ref[...]).astype(BF16)
    k_ref[...] = jnp.dot(u, wk_ref[...], preferred_element_type=F32)
    v_ref[...] = jnp.dot(u, wv_ref[...], preferred_element_type=F32)


def _memkv_call(mem, g, wk, wv, tm):
    B, M, _ = mem.shape
    x = mem.reshape(B * M, D_MODEL)
    rows = pl.BlockSpec((tm, D_MODEL), lambda i: (i, 0))
    full = lambda a: pl.BlockSpec(a.shape, lambda i: (0,) * a.ndim)
    k, v = pl.pallas_call(
        _memkv_kernel,
        grid=(B * M // tm,),
        in_specs=[rows, full(g), full(wk), full(wv)],
        out_specs=[rows, rows],
        out_shape=[jax.ShapeDtypeStruct((B * M, D_MODEL), F32)] * 2,
        compiler_params=pltpu.CompilerParams(dimension_semantics=("arbitrary",), vmem_limit_bytes=VMEM_LIMIT),
        name="mem_kv",
    )(x, g, wk, wv)
    return k.reshape(B, M, D_MODEL), v.reshape(B, M, D_MODEL)


def _pad_cols(w, n):
    return jnp.pad(w, ((0, 0), (0, n - w.shape[1])))


def _row(v):
    return v.reshape(1, -1).astype(F32)


def _layer_weights(l, P):
    w_in = P['w_in'][l]
    offs = [0]
    for s in _IN_SPLITS:
        offs.append(offs[-1] + s)
    z, xbc, dt, gq, gk, gv, ggate, glr, rf, rgate, merge = (w_in[:, offs[i]:offs[i + 1]] for i in range(len(_IN_SPLITS)))
    rep = lambda v: jnp.repeat(v, SSD_HEAD_DIM, axis=-1)
    zeros_lora = jnp.zeros((RWKV_LORA, D_MODEL), F32)
    lw = {
        'norm_mix': _row(P['norm_mix'][l]),
        'w_ssd': jnp.concatenate([z, xbc, rep(dt), _pad_cols(dt, LANE)], axis=1).astype(BF16),
        'ssd_conv_w': P['ssd_conv_w'][l].astype(F32),
        'ssd_conv_b': _row(P['ssd_conv_b'][l]),
        'ssd_dtb_x': _row(rep(P['ssd_dt_bias'][l])),
        'ssd_alog_x': _row(rep(P['ssd_A_log'][l])),
        'ssd_dtb_c': _pad_cols(_row(P['ssd_dt_bias'][l]), LANE),
        'ssd_alog_c': _pad_cols(_row(P['ssd_A_log'][l]), LANE),
        'ssd_d_x': _row(rep(P['ssd_D'][l])),
        'ssd_norm': _row(P['ssd_norm'][l]),
        'w_gla': jnp.concatenate([gq, gk, gv, ggate, _pad_cols(glr, LANE)], axis=1).astype(BF16),
        'gla_w2': jnp.pad(P['gla_gk_w2'][l], ((0, LANE - GLA_GATE_RANK), (0, 0))).astype(BF16),
        'gla_gb': _row(P['gla_gk_b'][l]),
        'gla_norm': _row(jnp.tile(P['gla_norm'][l], GLA_HEADS)),
        'w_rwkv': jnp.concatenate([rf, rgate], axis=1).astype(BF16),
        'rwkv_mu': _row(P['rwkv_mu'][l]),
        'rwkv_w0': _row(P['rwkv_w0'][l]),
        'rwkv_w2': jnp.concatenate([P['rwkv_w2'][l], zeros_lora], axis=0).astype(BF16),
        'rwkv_a0': _row(P['rwkv_a0'][l]),
        'rwkv_a2': jnp.concatenate([zeros_lora, P['rwkv_a2'][l]], axis=0).astype(BF16),
        'rwkv_k_k': _row(P['rwkv_k_k'][l]),
        'rwkv_k_a': _row(P['rwkv_k_a'][l]),
        'rwkv_r_k': _row(P['rwkv_r_k'][l]),
        'rwkv_ln_w': _row(P['rwkv_ln_w'][l]),
        'rwkv_ln_b': _row(P['rwkv_ln_b'][l]),
        'w_merge': merge.astype(BF16),
        'b_merge': _row(P['b_merge'][l]),
        'w_proj_ssd': P['w_proj_ssd'][l].astype(BF16),
        'w_proj_gla': P['w_proj_gla'][l].astype(BF16),
        'w_proj_rwkv': P['w_proj_rwkv'][l].astype(BF16),
        'w_out': P['w_out'][l].astype(BF16),
        'norm_xattn': _row(P['norm_xattn'][l]),
        'xa_wq': P['xa_wq'][l].astype(BF16),
        'xa_wo': P['xa_wo'][l].astype(BF16),
    }
    return lw


def _block_rows(L, target):
    return target if L % target == 0 else L


def _trunk(h, mem_k, mem_v, ssd_h, conv_buf, gla_h, rwkv_h, shift_buf, layers, norm_final):
    B, L, _ = h.shape
    tc = _block_rows(L, 256)
    q_gla = min(GLA_CHUNK, tc)
    q_rwkv = min(RWKV_CHUNK, tc)
    new = ([], [], [], [], [])
    y = None
    for l, lw in enumerate(layers):
        o_ssd, ssd_new, conv_new = _ssd_call(h, lw, ssd_h[l], conv_buf[l], tc)
        o_gla, gla_new = _gla_call(h, lw, gla_h[l], tc, q_gla)
        o_rwkv, rwkv_new, shift_new = _rwkv_call(h, lw, rwkv_h[l], shift_buf[l], tc, q_rwkv)
        h, y = _merge_call(h, o_ssd, o_gla, o_rwkv, mem_k[l], mem_v[l], lw, norm_final, tc)
        for lst, s_ in zip(new, (ssd_new, conv_new, gla_new, rwkv_new, shift_new)):
            lst.append(s_)
    return y, tuple(jnp.stack(lst, axis=0) for lst in new)


def kernel(x_prompt, x_sample, mem_prompt, state_ssd, state_ssd_conv, state_gla, state_rwkv, state_rwkv_shift,
           cache_mem_k, cache_mem_v, norm_mix, w_in, ssd_conv_w, ssd_conv_b, ssd_dt_bias, ssd_A_log, ssd_D,
           ssd_norm, w_proj_ssd, gla_gk_w2, gla_gk_b, gla_norm, w_proj_gla, rwkv_mu, rwkv_w0, rwkv_w2,
           rwkv_a0, rwkv_a2, rwkv_k_k, rwkv_k_a, rwkv_r_k, rwkv_ln_w, rwkv_ln_b, w_proj_rwkv, b_merge, w_out,
           norm_xattn, xa_wq, xa_wo, norm_mem, xa_wk, xa_wv, norm_final):
    P = dict(norm_mix=norm_mix, w_in=w_in, ssd_conv_w=ssd_conv_w, ssd_conv_b=ssd_conv_b, ssd_dt_bias=ssd_dt_bias,
             ssd_A_log=ssd_A_log, ssd_D=ssd_D, ssd_norm=ssd_norm, w_proj_ssd=w_proj_ssd, gla_gk_w2=gla_gk_w2,
             gla_gk_b=gla_gk_b, gla_norm=gla_norm, w_proj_gla=w_proj_gla, rwkv_mu=rwkv_mu, rwkv_w0=rwkv_w0,
             rwkv_w2=rwkv_w2, rwkv_a0=rwkv_a0, rwkv_a2=rwkv_a2, rwkv_k_k=rwkv_k_k, rwkv_k_a=rwkv_k_a,
             rwkv_r_k=rwkv_r_k, rwkv_ln_w=rwkv_ln_w, rwkv_ln_b=rwkv_ln_b, w_proj_rwkv=w_proj_rwkv,
             b_merge=b_merge, w_out=w_out, norm_xattn=norm_xattn, xa_wq=xa_wq, xa_wo=xa_wo)
    depth = w_in.shape[0]
    layers = [_layer_weights(l, P) for l in range(depth)]
    gf = _row(norm_final)

    bp, mem_len = mem_prompt.shape[0], mem_prompt.shape[1]
    kv = [_memkv_call(mem_prompt, _row(norm_mem[l]), xa_wk[l].astype(BF16), xa_wv[l].astype(BF16), 256)
          for l in range(depth)]
    mem_k_p = jnp.stack([t[0] for t in kv], axis=0)
    mem_v_p = jnp.stack([t[1] for t in kv], axis=0)

    zeros = lambda shape: jnp.zeros((depth, bp) + shape, F32)
    y_prompt, (p_ssd, p_conv, p_gla, p_rwkv, p_shift) = _trunk(
        x_prompt, mem_k_p, mem_v_p,
        zeros((SSD_HEADS, SSD_HEAD_DIM, SSD_STATE)), zeros((SSD_CONV - 1, SSD_CONV_DIM)),
        zeros((GLA_HEADS, GLA_DK, GLA_DV)), zeros((RWKV_HEADS, RWKV_HEAD, RWKV_HEAD)),
        zeros((1, RWKV_SHIFT_COLS)), layers, gf)

    bs = x_sample.shape[0]
    y_sample, (s_ssd, s_conv, s_gla, s_rwkv, s_shift) = _trunk(
        x_sample, cache_mem_k.reshape(depth, bs, mem_len, D_MODEL), cache_mem_v.reshape(depth, bs, mem_len, D_MODEL),
        state_ssd, state_ssd_conv, state_gla, state_rwkv, state_rwkv_shift, layers, gf)

    kv_shape = (depth, bp, mem_len, XA_HEADS, XA_HEAD_DIM)
    return (y_prompt, y_sample, p_ssd, p_conv, p_gla, p_rwkv, p_shift,
            mem_k_p.reshape(kv_shape), mem_v_p.reshape(kv_shape),
            s_ssd, s_conv, s_gla, s_rwkv, s_shift)
```

```python
import functools

import jax
import jax.numpy as jnp
from jax import lax
from jax.experimental import pallas as pl
from jax.experimental.pallas import tpu as pltpu

F32 = jnp.float32
BF16 = jnp.bfloat16

D_MODEL = 1024
NORM_EPS = 1e-5

SSD_HEADS = 16
SSD_HEAD_DIM = 64
SSD_GROUPS = 2
SSD_STATE = 128
SSD_CONV = 4
SSD_CONV_DIM = D_MODEL + 2 * SSD_GROUPS * SSD_STATE
SSD_GROUP_W = D_MODEL // SSD_GROUPS

GLA_HEADS = 4
GLA_DK = 128
GLA_DV = 256
GLA_KEY_DIM = GLA_HEADS * GLA_DK
GLA_GATE_RANK = 16
GLA_GATE_NORMALIZER = 16.0
GLA_CHUNK = 64

RWKV_HEADS = 16
RWKV_HEAD = 64
RWKV_LORA = 64
RWKV_SHIFT_COLS = 3 * D_MODEL + 2 * RWKV_LORA
RWKV_LN_EPS = 64e-5
RWKV_CHUNK = 64

XA_HEADS = 4
XA_HEAD_DIM = 256
N_BRANCHES = 3

LANE = 128
SEG_TILE = 256
VMEM_LIMIT = 56 * 1024 * 1024

_IN_SPLITS = (D_MODEL, SSD_CONV_DIM, SSD_HEADS, GLA_KEY_DIM, GLA_KEY_DIM, D_MODEL, D_MODEL,
              GLA_GATE_RANK, RWKV_SHIFT_COLS, D_MODEL, N_BRANCHES * D_MODEL)


def _mm(a, b):
    return jnp.dot(a.astype(BF16), b.astype(BF16), preferred_element_type=F32)


def _mm_nt(a, b):
    return lax.dot_general(a.astype(BF16), b.astype(BF16), (((1,), (1,)), ((), ())),
                           preferred_element_type=F32)


def _mm_tn(a, b):
    return lax.dot_general(a.astype(BF16), b.astype(BF16), (((0,), (0,)), ((), ())),
                           preferred_element_type=F32)


def _split3(x):
    hi = x.astype(BF16)
    r1 = x - hi.astype(F32)
    mid = r1.astype(BF16)
    lo = (r1 - mid.astype(F32)).astype(BF16)
    return hi, mid, lo


def _mm_exact_lhs(m_bf16, x):
    hi, mid, lo = _split3(x)
    dot = functools.partial(jnp.dot, preferred_element_type=F32)
    return dot(m_bf16, hi) + dot(m_bf16, mid) + dot(m_bf16, lo)


def _mm_exact_rhs(x, m_bf16):
    hi, mid, lo = _split3(x)
    dot = functools.partial(jnp.dot, preferred_element_type=F32)
    return dot(hi, m_bf16) + dot(mid, m_bf16) + dot(lo, m_bf16)


def _lower_tri(n, strict=False):
    r = lax.broadcasted_iota(jnp.int32, (n, n), 0)
    c = lax.broadcasted_iota(jnp.int32, (n, n), 1)
    return (r > c) if strict else (r >= c)


def _seg_ones(width, seg):
    r = lax.broadcasted_iota(jnp.int32, (width, width), 0) // seg
    c = lax.broadcasted_iota(jnp.int32, (width, width), 1) // seg
    return jnp.where(r == c, 1.0, 0.0).astype(BF16)


def _head_sums(x, seg):
    ones = _seg_ones(SEG_TILE, seg)
    parts = [_mm_exact_rhs(x[:, j:j + SEG_TILE], ones) for j in range(0, x.shape[1], SEG_TILE)]
    return jnp.concatenate(parts, axis=1)


def _rms(x, g):
    return x * lax.rsqrt(jnp.mean(x * x, axis=-1, keepdims=True) + NORM_EPS) * g


def _softplus(x):
    return jnp.maximum(x, 0.0) + jnp.log1p(jnp.exp(-jnp.abs(x)))


def _sigmoid(x):
    return 1.0 / (1.0 + jnp.exp(-x))


def _silu(x):
    return x * _sigmoid(x)


def _ssd_kernel(h_ref, g_ref, w_ref, convw_ref, convb_ref, dtbx_ref, alogx_ref, dtbc_ref, alogc_ref,
                dx_ref, gn_ref, st0_ref, conv0_ref,
                o_ref, st_out_ref, conv_out_ref,
                st_scr, xbuf_scr, *, tc):
    c = pl.program_id(1)
    nc = pl.num_programs(1)
    C = SSD_CONV_DIM
    W = D_MODEL

    @pl.when(c == 0)
    def _init():
        st_scr[...] = st0_ref[0].reshape(W, SSD_STATE).T
        xbuf_scr[0:8, :] = jnp.zeros((8, C), F32)
        xbuf_scr[5:8, :] = conv0_ref[0]

    u = _rms(h_ref[0], g_ref[...]).astype(BF16)
    z = jnp.dot(u, w_ref[:, 0:W], preferred_element_type=F32)
    xbuf_scr[8:8 + tc, :] = jnp.dot(u, w_ref[:, W:W + C], preferred_element_type=F32)
    dtx_raw = jnp.dot(u, w_ref[:, W + C:2 * W + C], preferred_element_type=F32)
    dtc_raw = jnp.dot(u, w_ref[:, 2 * W + C:2 * W + C + LANE], preferred_element_type=F32)

    conv = convb_ref[...] + convw_ref[SSD_CONV - 1:SSD_CONV, :] * xbuf_scr[8:8 + tc, :]
    for j in range(SSD_CONV - 1):
        conv = conv + convw_ref[j:j + 1, :] * xbuf_scr[5 + j:5 + j + tc, :]
    conv_out_ref[0] = xbuf_scr[tc + 5:tc + 8, :]
    xbuf_scr[0:8, :] = xbuf_scr[tc:tc + 8, :]

    xbc = _silu(conv)
    xs = xbc[:, 0:W]
    bm = xbc[:, W:W + SSD_GROUPS * SSD_STATE].astype(BF16)
    cm = xbc[:, W + SSD_GROUPS * SSD_STATE:C].astype(BF16)

    dtx = _softplus(dtx_raw + dtbx_ref[...])
    dtc = _softplus(dtc_raw + dtbc_ref[...])
    tri = jnp.where(_lower_tri(tc), 1.0, 0.0).astype(BF16)
    acs_x = _mm_exact_lhs(tri, dtx * -jnp.exp(alogx_ref[...]))
    acs_c = _mm_exact_lhs(tri, dtc * -jnp.exp(alogc_ref[...]))
    acs_ct = acs_c.T
    last = acs_x[tc - 1:tc, :]
    e_acs = jnp.exp(acs_x)
    xd = xs * dtx
    xd_b = xd.astype(BF16)
    xd_end = (xd * jnp.exp(last - acs_x)).astype(BF16)
    chunk_decay = jnp.exp(last)
    causal = _lower_tri(tc)

    ys = []
    for g in range(SSD_GROUPS):
        gs = slice(g * SSD_GROUP_W, (g + 1) * SSD_GROUP_W)
        bg = bm[:, g * SSD_STATE:(g + 1) * SSD_STATE]
        cg = cm[:, g * SSD_STATE:(g + 1) * SSD_STATE]
        cb = _mm_nt(cg, bg)
        st_g = st_scr[:, gs]
        y_off = _mm(cg, st_g) * e_acs[:, gs]
        parts = []
        for e in range(SSD_HEADS // SSD_GROUPS):
            hh = g * (SSD_HEADS // SSD_GROUPS) + e
            seg = acs_c[:, hh:hh + 1] - acs_ct[hh:hh + 1, :]
            lmat = jnp.exp(jnp.where(causal, seg, -jnp.inf))
            parts.append(_mm(cb * lmat, xd_b[:, hh * SSD_HEAD_DIM:(hh + 1) * SSD_HEAD_DIM]))
        ys.append(jnp.concatenate(parts, axis=1) + y_off)
        st_scr[:, gs] = st_g * chunk_decay[:, gs] + _mm_tn(bg, xd_end[:, gs])
    y = jnp.concatenate(ys, axis=1) + xs * dx_ref[...]

    t = y * _silu(z)
    outs = []
    for g in range(SSD_GROUPS):
        tg = t[:, g * SSD_GROUP_W:(g + 1) * SSD_GROUP_W]
        outs.append(tg * lax.rsqrt(jnp.mean(tg * tg, axis=-1, keepdims=True) + NORM_EPS))
    o_ref[0] = (jnp.concatenate(outs, axis=1) * gn_ref[...]).astype(o_ref.dtype)

    @pl.when(c == nc - 1)
    def _fin():
        st_out_ref[0] = st_scr[...].T.reshape(SSD_HEADS, SSD_HEAD_DIM, SSD_STATE)


def _const_spec(shape):
    nd = len(shape)
    return pl.BlockSpec(shape, lambda b, c: (0,) * nd)


def _ssd_call(h, lw, st0, conv0, tc):
    B, L, _ = h.shape
    nc = L // tc
    consts = [lw['norm_mix'], lw['w_ssd'], lw['ssd_conv_w'], lw['ssd_conv_b'], lw['ssd_dtb_x'], lw['ssd_alog_x'],
              lw['ssd_dtb_c'], lw['ssd_alog_c'], lw['ssd_d_x'], lw['ssd_norm']]
    in_specs = ([pl.BlockSpec((1, tc, D_MODEL), lambda b, c: (b, c, 0))]
                + [_const_spec(a.shape) for a in consts]
                + [pl.BlockSpec((1, SSD_HEADS, SSD_HEAD_DIM, SSD_STATE), lambda b, c: (b, 0, 0, 0)),
                   pl.BlockSpec((1, SSD_CONV - 1, SSD_CONV_DIM), lambda b, c: (b, 0, 0))])
    out_specs = [pl.BlockSpec((1, tc, D_MODEL), lambda b, c: (b, c, 0)),
                 pl.BlockSpec((1, SSD_HEADS, SSD_HEAD_DIM, SSD_STATE), lambda b, c: (b, 0, 0, 0)),
                 pl.BlockSpec((1, SSD_CONV - 1, SSD_CONV_DIM), lambda b, c: (b, 0, 0))]
    out_shape = [jax.ShapeDtypeStruct((B, L, D_MODEL), BF16),
                 jax.ShapeDtypeStruct((B, SSD_HEADS, SSD_HEAD_DIM, SSD_STATE), F32),
                 jax.ShapeDtypeStruct((B, SSD_CONV - 1, SSD_CONV_DIM), F32)]
    return pl.pallas_call(
        functools.partial(_ssd_kernel, tc=tc),
        grid=(B, nc), in_specs=in_specs, out_specs=out_specs, out_shape=out_shape,
        scratch_shapes=[pltpu.VMEM((SSD_STATE, D_MODEL), F32), pltpu.VMEM((tc + 8, SSD_CONV_DIM), F32)],
        compiler_params=pltpu.CompilerParams(dimension_semantics=("arbitrary", "arbitrary"),
                                             vmem_limit_bytes=VMEM_LIMIT),
        name="ssd_mixer",
    )(h, *consts, st0, conv0)


def _gla_kernel(h_ref, g_ref, w_ref, w2_ref, gb_ref, gn_ref, s0_ref,
                o_ref, s_out_ref,
                s_scr, q_scr, k_scr, v_scr, gl_scr, o_scr, *, tc, q):
    c = pl.program_id(1)
    nc = pl.num_programs(1)
    KD = GLA_KEY_DIM
    W = D_MODEL

    @pl.when(c == 0)
    def _init():
        for hd in range(GLA_HEADS):
            s_scr[hd] = s0_ref[0, hd].T

    u = _rms(h_ref[0], g_ref[...]).astype(BF16)
    q_scr[...] = jnp.dot(u, w_ref[:, 0:KD], preferred_element_type=F32) * (GLA_DK ** -0.5)
    k_scr[...] = jnp.dot(u, w_ref[:, KD:2 * KD], preferred_element_type=F32)
    v_scr[...] = jnp.dot(u, w_ref[:, 2 * KD:2 * KD + W], preferred_element_type=F32)
    glr = jnp.dot(u, w_ref[:, 2 * KD + 2 * W:2 * KD + 2 * W + LANE], preferred_element_type=F32)
    gpre = _mm(glr, w2_ref[...]) + gb_ref[...]
    gl_scr[...] = -_softplus(-gpre) / GLA_GATE_NORMALIZER

    tri = jnp.where(_lower_tri(q), 1.0, 0.0).astype(BF16)
    causal = _lower_tri(q)

    def chunk(j, carry):
        rows = pl.ds(pl.multiple_of(j * q, q), q)
        gcs = _mm_exact_lhs(tri, gl_scr[rows, :])
        last = gcs[q - 1:q, :]
        kk = k_scr[rows, :]
        qe = (q_scr[rows, :] * jnp.exp(gcs)).astype(BF16)
        ke = (kk * jnp.exp(-gcs)).astype(BF16)
        kend = (kk * jnp.exp(last - gcs)).astype(BF16)
        dec = jnp.exp(last)
        vv = v_scr[rows, :].astype(BF16)
        for hd in range(GLA_HEADS):
            ks = slice(hd * GLA_DK, (hd + 1) * GLA_DK)
            vs = slice(hd * GLA_DV, (hd + 1) * GLA_DV)
            a = jnp.where(causal, _mm_nt(qe[:, ks], ke[:, ks]), 0.0)
            s_t = s_scr[hd]
            o_scr[rows, vs] = _mm(a, vv[:, vs]) + _mm_nt(qe[:, ks], s_t)
            s_scr[hd] = s_t * dec[:, ks] + _mm_tn(vv[:, vs], kend[:, ks])
        return carry

    lax.fori_loop(0, tc // q, chunk, 0)

    gate = _silu(jnp.dot(u, w_ref[:, 2 * KD + W:2 * KD + 2 * W], preferred_element_type=F32))
    o = o_scr[...]
    outs = []
    for hd in range(GLA_HEADS):
        oh = o[:, hd * GLA_DV:(hd + 1) * GLA_DV]
        outs.append(oh * lax.rsqrt(jnp.mean(oh * oh, axis=-1, keepdims=True) + NORM_EPS))
    o_ref[0] = (jnp.concatenate(outs, axis=1) * gn_ref[...] * gate).astype(o_ref.dtype)

    @pl.when(c == nc - 1)
    def _fin():
        for hd in range(GLA_HEADS):
            s_out_ref[0, hd] = s_scr[hd].T


def _gla_call(h, lw, s0, tc, q):
    B, L, _ = h.shape
    nc = L // tc
    consts = [lw['norm_mix'], lw['w_gla'], lw['gla_w2'], lw['gla_gb'], lw['gla_norm']]
    in_specs = ([pl.BlockSpec((1, tc, D_MODEL), lambda b, c: (b, c, 0))]
                + [_const_spec(a.shape) for a in consts]
                + [pl.BlockSpec((1, GLA_HEADS, GLA_DK, GLA_DV), lambda b, c: (b, 0, 0, 0))])
    out_specs = [pl.BlockSpec((1, tc, D_MODEL), lambda b, c: (b, c, 0)),
                 pl.BlockSpec((1, GLA_HEADS, GLA_DK, GLA_DV), lambda b, c: (b, 0, 0, 0))]
    out_shape = [jax.ShapeDtypeStruct((B, L, D_MODEL), BF16),
                 jax.ShapeDtypeStruct((B, GLA_HEADS, GLA_DK, GLA_DV), F32)]
    return pl.pallas_call(
        functools.partial(_gla_kernel, tc=tc, q=q),
        grid=(B, nc), in_specs=in_specs, out_specs=out_specs, out_shape=out_shape,
        scratch_shapes=[pltpu.VMEM((GLA_HEADS, GLA_DV, GLA_DK), F32),
                        pltpu.VMEM((tc, GLA_KEY_DIM), F32), pltpu.VMEM((tc, GLA_KEY_DIM), F32),
                        pltpu.VMEM((tc, D_MODEL), F32), pltpu.VMEM((tc, GLA_KEY_DIM), F32),
                        pltpu.VMEM((tc, D_MODEL), F32)],
        compiler_params=pltpu.CompilerParams(dimension_semantics=("arbitrary", "arbitrary"),
                                             vmem_limit_bytes=VMEM_LIMIT),
        name="gla_mixer",
    )(h, *consts, s0)


def _solve_unit_lower_all(ns, zs, steps):
    w = ns[0].shape[1]
    wy = w + zs[0].shape[1]
    right = lax.broadcasted_iota(jnp.int32, (ns[0].shape[0], wy), 1) >= w
    ys = [jnp.concatenate([n, z], axis=1) for n, z in zip(ns, zs)]
    ps = list(ns)
    for _ in range(steps):
        pys = [_mm(p, y) for p, y in zip(ps, ys)]
        ys = [py + jnp.where(right, y, 0.0) for py, y in zip(pys, ys)]
        ps = [py[:, 0:w] for py in pys]
    return [y[:, w:wy] for y in ys]


def _rwkv_kernel(h_ref, g_ref, w_ref, mu_ref, w0_ref, w2_ref, a0_ref, a2_ref, kk_ref, ka_ref, rk_ref,
                 lnw_ref, lnb_ref, s0_ref, sh0_ref,
                 o_ref, s_out_ref, sh_out_ref,
                 s_scr, carry_scr, r_scr, k_scr, v_scr, p_scr, q_scr, lw_scr, o_scr, *, tc, q):
    c = pl.program_id(1)
    nc = pl.num_programs(1)
    W = D_MODEL
    N = RWKV_HEAD

    @pl.when(c == 0)
    def _init():
        s_scr[...] = s0_ref[0]
        carry_scr[...] = sh0_ref[0]

    u = _rms(h_ref[0], g_ref[...]).astype(BF16)
    rf = jnp.dot(u, w_ref[:, 0:RWKV_SHIFT_COLS], preferred_element_type=F32)
    first = lax.broadcasted_iota(jnp.int32, (tc, 1), 0) == 0
    prev = jnp.where(first, carry_scr[...], pltpu.roll(rf, 1, axis=0))
    carry_scr[...] = rf[tc - 1:tc, :]
    sh_out_ref[0] = rf[tc - 1:tc, :]
    rfm = rf + (prev - rf) * mu_ref[...]
    r7 = rfm[:, 0:W]
    k7 = rfm[:, W:2 * W]
    v7 = rfm[:, 2 * W:3 * W]
    lora = rfm[:, 3 * W:3 * W + 2 * RWKV_LORA]
    w_pre = w0_ref[...] + _mm(jnp.tanh(lora), w2_ref[...])
    lw_scr[...] = -jnp.exp(-_softplus(-w_pre) - 0.5)
    a = _sigmoid(a0_ref[...] + _mm(lora, a2_ref[...]))
    kkf = k7 * kk_ref[...]
    kk = kkf / jnp.maximum(jnp.sqrt(_head_sums(kkf * kkf, N)), 1e-12)
    k7 = k7 * (1.0 + (a - 1.0) * ka_ref[...])
    bonus = _head_sums(r7 * k7 * rk_ref[...], N) * v7
    r_scr[...] = r7
    k_scr[...] = k7
    v_scr[...] = v7
    p_scr[...] = -kk
    q_scr[...] = kk * a

    tri = jnp.where(_lower_tri(q), 1.0, 0.0).astype(BF16)
    strict = _lower_tri(q, strict=True)
    row2 = lax.broadcasted_iota(jnp.int32, (q, 2 * q), 0)
    col2 = lax.broadcasted_iota(jnp.int32, (q, 2 * q), 1)
    incl2 = row2 >= jnp.where(col2 >= q, col2 - q, col2)
    strict_r = (col2 >= q) & (row2 > col2 - q)
    steps = max(1, (q - 1).bit_length())

    def chunk(j, carry):
        rows = pl.ds(pl.multiple_of(j * q, q), q)
        lw = lw_scr[rows, :]
        cum = _mm_exact_lhs(tri, lw)
        last = cum[q - 1:q, :]
        e_in = jnp.exp(cum)
        e_inv = jnp.exp(-cum)
        e_end = jnp.exp(last - cum)
        k_all = k_scr[rows, :]
        q_all = q_scr[rows, :]
        rt = (r_scr[rows, :] * e_in).astype(BF16)
        pt = (p_scr[rows, :] * jnp.exp(cum - lw)).astype(BF16)
        qt = (q_all * e_inv).astype(BF16)
        kt = (k_all * e_inv).astype(BF16)
        qend = (q_all * e_end).astype(BF16)
        kend = (k_all * e_end).astype(BF16)
        dec = jnp.exp(last)
        vv = v_scr[rows, :].astype(BF16)
        heads = [slice(hd * N, (hd + 1) * N) for hd in range(RWKV_HEADS)]
        s0s = [s_scr[hd] for hd in range(RWKV_HEADS)]
        prs = [jnp.concatenate([pt[:, hs], rt[:, hs]], axis=0) for hs in heads]
        amats = [_mm_nt(pr, jnp.concatenate([qt[:, hs], kt[:, hs]], axis=0))
                 for pr, hs in zip(prs, heads)]
        pss = [_mm_nt(pr, s0) for pr, s0 in zip(prs, s0s)]
        vhs = [vv[:, hs] for hs in heads]
        zs = [ps[0:q] + _mm(jnp.where(strict_r, am[0:q], 0.0), jnp.concatenate([vh, vh], axis=0))
              for ps, am, vh in zip(pss, amats, vhs)]
        ns = [jnp.where(strict, am[0:q, 0:q], 0.0) for am in amats]
        us = _solve_unit_lower_all(ns, zs, steps)
        uvs = [jnp.concatenate([uu.astype(BF16), vh], axis=0) for uu, vh in zip(us, vhs)]
        for hd, hs in enumerate(heads):
            o_scr[rows, hs] = pss[hd][q:2 * q] + _mm(jnp.where(incl2, amats[hd][q:2 * q], 0.0), uvs[hd])
        for hd, hs in enumerate(heads):
            ends = jnp.concatenate([qend[:, hs], kend[:, hs]], axis=0)
            s_scr[hd] = s0s[hd] * dec[:, hs] + _mm_tn(uvs[hd], ends)
        return carry

    lax.fori_loop(0, tc // q, chunk, 0)

    o7 = o_scr[...]
    mean = _head_sums(o7, N) * (1.0 / N)
    d = o7 - mean
    var = _head_sums(d * d, N) * (1.0 / N)
    ln = d * lax.rsqrt(var + RWKV_LN_EPS) * lnw_ref[...] + lnb_ref[...]
    gate = _silu(jnp.dot(u, w_ref[:, RWKV_SHIFT_COLS:RWKV_SHIFT_COLS + W], preferred_element_type=F32))
    o_ref[0] = ((ln + bonus) * gate).astype(o_ref.dtype)

    @pl.when(c == nc - 1)
    def _fin():
        s_out_ref[0] = s_scr[...]


def _rwkv_call(h, lw, s0, sh0, tc, q):
    B, L, _ = h.shape
    nc = L // tc
    consts = [lw['norm_mix'], lw['w_rwkv'], lw['rwkv_mu'], lw['rwkv_w0'], lw['rwkv_w2'], lw['rwkv_a0'], lw['rwkv_a2'],
              lw['rwkv_k_k'], lw['rwkv_k_a'], lw['rwkv_r_k'], lw['rwkv_ln_w'], lw['rwkv_ln_b']]
    in_specs = ([pl.BlockSpec((1, tc, D_MODEL), lambda b, c: (b, c, 0))]
                + [_const_spec(a.shape) for a in consts]
                + [pl.BlockSpec((1, RWKV_HEADS, RWKV_HEAD, RWKV_HEAD), lambda b, c: (b, 0, 0, 0)),
                   pl.BlockSpec((1, 1, RWKV_SHIFT_COLS), lambda b, c: (b, 0, 0))])
    out_specs = [pl.BlockSpec((1, tc, D_MODEL), lambda b, c: (b, c, 0)),
                 pl.BlockSpec((1, RWKV_HEADS, RWKV_HEAD, RWKV_HEAD), lambda b, c: (b, 0, 0, 0)),
                 pl.BlockSpec((1, 1, RWKV_SHIFT_COLS), lambda b, c: (b, 0, 0))]
    out_shape = [jax.ShapeDtypeStruct((B, L, D_MODEL), BF16),
                 jax.ShapeDtypeStruct((B, RWKV_HEADS, RWKV_HEAD, RWKV_HEAD), F32),
                 jax.ShapeDtypeStruct((B, 1, RWKV_SHIFT_COLS), F32)]
    big = pltpu.VMEM((tc, D_MODEL), F32)
    return pl.pallas_call(
        functools.partial(_rwkv_kernel, tc=tc, q=q),
        grid=(B, nc), in_specs=in_specs, out_specs=out_specs, out_shape=out_shape,
        scratch_shapes=[pltpu.VMEM((RWKV_HEADS, RWKV_HEAD, RWKV_HEAD), F32),
                        pltpu.VMEM((1, RWKV_SHIFT_COLS), F32),
                        big, big, big, big, big, big, big],
        compiler_params=pltpu.CompilerParams(dimension_semantics=("arbitrary", "arbitrary"),
                                             vmem_limit_bytes=VMEM_LIMIT),
        name="rwkv_mixer",
    )(h, *consts, s0, sh0)


def _merge_kernel(h_ref, os_ref, og_ref, or_ref, mk_ref, mv_ref,
                  g_ref, wm_ref, bm_ref, ps_ref, pg_ref, pr_ref, wo_ref, gx_ref, xq_ref, xo_ref, gf_ref,
                  h_out_ref, y_ref):
    W = D_MODEL
    h = h_ref[0]
    u = _rms(h, g_ref[...]).astype(BF16)
    m = None
    for i, (branch_ref, proj_ref) in enumerate(((os_ref, ps_ref), (og_ref, pg_ref), (or_ref, pr_ref))):
        s = _sigmoid(jnp.dot(u, wm_ref[:, i * W:(i + 1) * W], preferred_element_type=F32)
                     + bm_ref[:, i * W:(i + 1) * W])
        term = s * jnp.dot(branch_ref[0], proj_ref[...], preferred_element_type=F32)
        m = term if m is None else m + term
    h1 = h + _mm(m, wo_ref[...])

    u2 = _rms(h1, gx_ref[...])
    qx = _mm(u2, xq_ref[...])
    outs = []
    for hd in range(XA_HEADS):
        hs = slice(hd * XA_HEAD_DIM, (hd + 1) * XA_HEAD_DIM)
        sc = _mm_nt(qx[:, hs], mk_ref[0, :, hs]) * (XA_HEAD_DIM ** -0.5)
        sc = sc - jnp.max(sc, axis=-1, keepdims=True)
        e = jnp.exp(sc)
        p = e / jnp.sum(e, axis=-1, keepdims=True)
        outs.append(_mm(p, mv_ref[0, :, hs]))
    h2 = h1 + _mm(jnp.concatenate(outs, axis=1), xo_ref[...])
    h_out_ref[0] = h2
    y_ref[0] = _rms(h2, gf_ref[...])


def _merge_call(h, o_ssd, o_gla, o_rwkv, mem_k, mem_v, lw, norm_final, tm):
    B, L, _ = h.shape
    M = mem_k.shape[1]
    consts = [lw['norm_mix'], lw['w_merge'], lw['b_merge'], lw['w_proj_ssd'], lw['w_proj_gla'], lw['w_proj_rwkv'],
              lw['w_out'], lw['norm_xattn'], lw['xa_wq'], lw['xa_wo'], norm_final]
    tok = pl.BlockSpec((1, tm, D_MODEL), lambda b, c: (b, c, 0))
    mem = pl.BlockSpec((1, M, D_MODEL), lambda b, c: (b, 0, 0))
    return pl.pallas_call(
        _merge_kernel,
        grid=(B, L // tm),
        in_specs=[tok, tok, tok, tok, mem, mem] + [_const_spec(a.shape) for a in consts],
        out_specs=[tok, tok],
        out_shape=[jax.ShapeDtypeStruct((B, L, D_MODEL), F32), jax.ShapeDtypeStruct((B, L, D_MODEL), F32)],
        compiler_params=pltpu.CompilerParams(dimension_semantics=("arbitrary", "arbitrary"),
                                             vmem_limit_bytes=VMEM_LIMIT),
        name="merge_xattn",
    )(h, o_ssd, o_gla, o_rwkv, mem_k, mem_v, *consts)


def _memkv_kernel(x_ref, g_ref, wk_ref, wv_ref, k_ref, v_ref):
    u = _rms(x_ref[...], g_ref[...]).astype(BF16)
    k_ref[...] = jnp.dot(u, wk_ref[...], preferred_element_type=F32)
    v_ref[...] = jnp.dot(u, wv_ref[...], preferred_element_type=F32)


def _memkv_call(mem, g, wk, wv, tm):
    B, M, _ = mem.shape
    x = mem.reshape(B * M, D_MODEL)
    rows = pl.BlockSpec((tm, D_MODEL), lambda i: (i, 0))
    full = lambda a: pl.BlockSpec(a.shape, lambda i: (0,) * a.ndim)
    k, v = pl.pallas_call(
        _memkv_kernel,
        grid=(B * M // tm,),
        in_specs=[rows, full(g), full(wk), full(wv)],
        out_specs=[rows, rows],
        out_shape=[jax.ShapeDtypeStruct((B * M, D_MODEL), F32)] * 2,
        compiler_params=pltpu.CompilerParams(dimension_semantics=("arbitrary",), vmem_limit_bytes=VMEM_LIMIT),
        name="mem_kv",
    )(x, g, wk, wv)
    return k.reshape(B, M, D_MODEL), v.reshape(B, M, D_MODEL)


def _pad_cols(w, n):
    return jnp.pad(w, ((0, 0), (0, n - w.shape[1])))


def _row(v):
    return v.reshape(1, -1).astype(F32)


def _layer_weights(l, P):
    w_in = P['w_in'][l]
    offs = [0]
    for s in _IN_SPLITS:
        offs.append(offs[-1] + s)
    z, xbc, dt, gq, gk, gv, ggate, glr, rf, rgate, merge = (w_in[:, offs[i]:offs[i + 1]] for i in range(len(_IN_SPLITS)))
    rep = lambda v: jnp.repeat(v, SSD_HEAD_DIM, axis=-1)
    zeros_lora = jnp.zeros((RWKV_LORA, D_MODEL), F32)
    lw = {
        'norm_mix': _row(P['norm_mix'][l]),
        'w_ssd': jnp.concatenate([z, xbc, rep(dt), _pad_cols(dt, LANE)], axis=1).astype(BF16),
        'ssd_conv_w': P['ssd_conv_w'][l].astype(F32),
        'ssd_conv_b': _row(P['ssd_conv_b'][l]),
        'ssd_dtb_x': _row(rep(P['ssd_dt_bias'][l])),
        'ssd_alog_x': _row(rep(P['ssd_A_log'][l])),
        'ssd_dtb_c': _pad_cols(_row(P['ssd_dt_bias'][l]), LANE),
        'ssd_alog_c': _pad_cols(_row(P['ssd_A_log'][l]), LANE),
        'ssd_d_x': _row(rep(P['ssd_D'][l])),
        'ssd_norm': _row(P['ssd_norm'][l]),
        'w_gla': jnp.concatenate([gq, gk, gv, ggate, _pad_cols(glr, LANE)], axis=1).astype(BF16),
        'gla_w2': jnp.pad(P['gla_gk_w2'][l], ((0, LANE - GLA_GATE_RANK), (0, 0))).astype(BF16),
        'gla_gb': _row(P['gla_gk_b'][l]),
        'gla_norm': _row(jnp.tile(P['gla_norm'][l], GLA_HEADS)),
        'w_rwkv': jnp.concatenate([rf, rgate], axis=1).astype(BF16),
        'rwkv_mu': _row(P['rwkv_mu'][l]),
        'rwkv_w0': _row(P['rwkv_w0'][l]),
        'rwkv_w2': jnp.concatenate([P['rwkv_w2'][l], zeros_lora], axis=0).astype(BF16),
        'rwkv_a0': _row(P['rwkv_a0'][l]),
        'rwkv_a2': jnp.concatenate([zeros_lora, P['rwkv_a2'][l]], axis=0).astype(BF16),
        'rwkv_k_k': _row(P['rwkv_k_k'][l]),
        'rwkv_k_a': _row(P['rwkv_k_a'][l]),
        'rwkv_r_k': _row(P['rwkv_r_k'][l]),
        'rwkv_ln_w': _row(P['rwkv_ln_w'][l]),
        'rwkv_ln_b': _row(P['rwkv_ln_b'][l]),
        'w_merge': merge.astype(BF16),
        'b_merge': _row(P['b_merge'][l]),
        'w_proj_ssd': P['w_proj_ssd'][l].astype(BF16),
        'w_proj_gla': P['w_proj_gla'][l].astype(BF16),
        'w_proj_rwkv': P['w_proj_rwkv'][l].astype(BF16),
        'w_out': P['w_out'][l].astype(BF16),
        'norm_xattn': _row(P['norm_xattn'][l]),
        'xa_wq': P['xa_wq'][l].astype(BF16),
        'xa_wo': P['xa_wo'][l].astype(BF16),
    }
    return lw


def _block_rows(L, target):
    return target if L % target == 0 else L


def _trunk(h, mem_k, mem_v, ssd_h, conv_buf, gla_h, rwkv_h, shift_buf, layers, norm_final):
    B, L, _ = h.shape
    tc = _block_rows(L, 256)
    q_gla = min(GLA_CHUNK, tc)
    q_rwkv = min(RWKV_CHUNK, tc)
    new = ([], [], [], [], [])
    y = None
    for l, lw in enumerate(layers):
        o_ssd, ssd_new, conv_new = _ssd_call(h, lw, ssd_h[l], conv_buf[l], tc)
        o_gla, gla_new = _gla_call(h, lw, gla_h[l], tc, q_gla)
        o_rwkv, rwkv_new, shift_new = _rwkv_call(h, lw, rwkv_h[l], shift_buf[l], tc, q_rwkv)
        h, y = _merge_call(h, o_ssd, o_gla, o_rwkv, mem_k[l], mem_v[l], lw, norm_final, tc)
        for lst, s_ in zip(new, (ssd_new, conv_new, gla_new, rwkv_new, shift_new)):
            lst.append(s_)
    return y, tuple(jnp.stack(lst, axis=0) for lst in new)


def kernel(x_prompt, x_sample, mem_prompt, state_ssd, state_ssd_conv, state_gla, state_rwkv, state_rwkv_shift,
           cache_mem_k, cache_mem_v, norm_mix, w_in, ssd_conv_w, ssd_conv_b, ssd_dt_bias, ssd_A_log, ssd_D,
           ssd_norm, w_proj_ssd, gla_gk_w2, gla_gk_b, gla_norm, w_proj_gla, rwkv_mu, rwkv_w0, rwkv_w2,
           rwkv_a0, rwkv_a2, rwkv_k_k, rwkv_k_a, rwkv_r_k, rwkv_ln_w, rwkv_ln_b, w_proj_rwkv, b_merge, w_out,
           norm_xattn, xa_wq, xa_wo, norm_mem, xa_wk, xa_wv, norm_final):
    P = dict(norm_mix=norm_mix, w_in=w_in, ssd_conv_w=ssd_conv_w, ssd_conv_b=ssd_conv_b, ssd_dt_bias=ssd_dt_bias,
             ssd_A_log=ssd_A_log, ssd_D=ssd_D, ssd_norm=ssd_norm, w_proj_ssd=w_proj_ssd, gla_gk_w2=gla_gk_w2,
             gla_gk_b=gla_gk_b, gla_norm=gla_norm, w_proj_gla=w_proj_gla, rwkv_mu=rwkv_mu, rwkv_w0=rwkv_w0,
             rwkv_w2=rwkv_w2, rwkv_a0=rwkv_a0, rwkv_a2=rwkv_a2, rwkv_k_k=rwkv_k_k, rwkv_k_a=rwkv_k_a,
             rwkv_r_k=rwkv_r_k, rwkv_ln_w=rwkv_ln_w, rwkv_ln_b=rwkv_ln_b, w_proj_rwkv=w_proj_rwkv,
             b_merge=b_merge, w_out=w_out, norm_xattn=norm_xattn, xa_wq=xa_wq, xa_wo=xa_wo)
    depth = w_in.shape[0]
    layers = [_layer_weights(l, P) for l in range(depth)]
    gf = _row(norm_final)

    bp, mem_len = mem_prompt.shape[0], mem_prompt.shape[1]
    kv = [_memkv_call(mem_prompt, _row(norm_mem[l]), xa_wk[l].astype(BF16), xa_wv[l].astype(BF16), 256)
          for l in range(depth)]
    mem_k_p = jnp.stack([t[0] for t in kv], axis=0)
    mem_v_p = jnp.stack([t[1] for t in kv], axis=0)

    zeros = lambda shape: jnp.zeros((depth, bp) + shape, F32)
    y_prompt, (p_ssd, p_conv, p_gla, p_rwkv, p_shift) = _trunk(
        x_prompt, mem_k_p, mem_v_p,
        zeros((SSD_HEADS, SSD_HEAD_DIM, SSD_STATE)), zeros((SSD_CONV - 1, SSD_CONV_DIM)),
        zeros((GLA_HEADS, GLA_DK, GLA_DV)), zeros((RWKV_HEADS, RWKV_HEAD, RWKV_HEAD)),
        zeros((1, RWKV_SHIFT_COLS)), layers, gf)

    bs = x_sample.shape[0]
    y_sample, (s_ssd, s_conv, s_gla, s_rwkv, s_shift) = _trunk(
        x_sample, cache_mem_k.reshape(depth, bs, mem_len, D_MODEL), cache_mem_v.reshape(depth, bs, mem_len, D_MODEL),
        state_ssd, state_ssd_conv, state_gla, state_rwkv, state_rwkv_shift, layers, gf)

    kv_shape = (depth, bp, mem_len, XA_HEADS, XA_HEAD_DIM)
    return (y_prompt, y_sample, p_ssd, p_conv, p_gla, p_rwkv, p_shift,
            mem_k_p.reshape(kv_shape), mem_v_p.reshape(kv_shape),
            s_ssd, s_conv, s_gla, s_rwkv, s_shift)
```

```python
import functools

import jax
import jax.numpy as jnp
import numpy as np
from jax import lax
from jax.experimental import pallas as pl
from jax.experimental.pallas import tpu as pltpu

F32 = jnp.float32
BF16 = jnp.bfloat16

D_MODEL = 1024
NORM_EPS = 1e-5

SSD_HEADS = 16
SSD_HEAD_DIM = 64
SSD_GROUPS = 2
SSD_STATE = 128
SSD_CONV = 4
SSD_CONV_DIM = D_MODEL + 2 * SSD_GROUPS * SSD_STATE
SSD_GROUP_W = D_MODEL // SSD_GROUPS

GLA_HEADS = 4
GLA_DK = 128
GLA_DV = 256
GLA_KEY_DIM = GLA_HEADS * GLA_DK
GLA_GATE_RANK = 16
GLA_GATE_NORMALIZER = 16.0
GLA_CHUNK = 64

RWKV_HEADS = 16
RWKV_HEAD = 64
RWKV_LORA = 64
RWKV_SHIFT_COLS = 3 * D_MODEL + 2 * RWKV_LORA
RWKV_LN_EPS = 64e-5
RWKV_CHUNK = 64

XA_HEADS = 4
XA_HEAD_DIM = 256
N_BRANCHES = 3

LANE = 128
SEG_TILE = 256
VMEM_LIMIT = 56 * 1024 * 1024

SSD_BLOCK_ROWS = 256
GLA_BLOCK_ROWS = 512
RWKV_BLOCK_ROWS = 256
MERGE_BLOCK_ROWS = 512

_IN_SPLITS = (D_MODEL, SSD_CONV_DIM, SSD_HEADS, GLA_KEY_DIM, GLA_KEY_DIM, D_MODEL, D_MODEL,
              GLA_GATE_RANK, RWKV_SHIFT_COLS, D_MODEL, N_BRANCHES * D_MODEL)


def _mm(a, b):
    return jnp.dot(a.astype(BF16), b.astype(BF16), preferred_element_type=F32)


def _mm_nt(a, b):
    return lax.dot_general(a.astype(BF16), b.astype(BF16), (((1,), (1,)), ((), ())),
                           preferred_element_type=F32)


def _mm_tn(a, b):
    return lax.dot_general(a.astype(BF16), b.astype(BF16), (((0,), (0,)), ((), ())),
                           preferred_element_type=F32)


def _split2(x):
    hi = x.astype(BF16)
    lo = (x - hi.astype(F32)).astype(BF16)
    return hi, lo


def _mm_exact_lhs(m_bf16, x):
    hi, lo = _split2(x)
    dot = functools.partial(jnp.dot, preferred_element_type=F32)
    return dot(m_bf16, hi) + dot(m_bf16, lo)


def _mm_exact_rhs(x, m_bf16):
    hi, lo = _split2(x)
    dot = functools.partial(jnp.dot, preferred_element_type=F32)
    return dot(hi, m_bf16) + dot(lo, m_bf16)


def _lower_tri(n, strict=False):
    r = lax.broadcasted_iota(jnp.int32, (n, n), 0)
    c = lax.broadcasted_iota(jnp.int32, (n, n), 1)
    return (r > c) if strict else (r >= c)


def _seg_ones(width, seg):
    r = lax.broadcasted_iota(jnp.int32, (width, width), 0) // seg
    c = lax.broadcasted_iota(jnp.int32, (width, width), 1) // seg
    return jnp.where(r == c, 1.0, 0.0).astype(BF16)


def _head_sums(x, seg):
    ones = _seg_ones(SEG_TILE, seg)
    parts = [_mm_exact_rhs(x[:, j:j + SEG_TILE], ones) for j in range(0, x.shape[1], SEG_TILE)]
    return jnp.concatenate(parts, axis=1)


def _rms(x, g):
    return x * lax.rsqrt(jnp.mean(x * x, axis=-1, keepdims=True) + NORM_EPS) * g


def _softplus(x):
    return jnp.maximum(x, 0.0) + jnp.log1p(jnp.exp(-jnp.abs(x)))


def _sigmoid(x):
    return 1.0 / (1.0 + jnp.exp(-x))


def _silu(x):
    return x * _sigmoid(x)


def _ssd_kernel(h_ref, g_ref, w_ref, convw_ref, convb_ref, dtbx_ref, alogx_ref, dtbc_ref, alogc_ref,
                dx_ref, gn_ref, st0_ref, conv0_ref,
                o_ref, st_out_ref, conv_out_ref,
                st_scr, xbuf_scr, *, tc):
    c = pl.program_id(1)
    nc = pl.num_programs(1)
    C = SSD_CONV_DIM
    W = D_MODEL

    @pl.when(c == 0)
    def _init():
        st_scr[...] = st0_ref[0].reshape(W, SSD_STATE).T
        xbuf_scr[0:8, :] = jnp.zeros((8, C), F32)
        xbuf_scr[5:8, :] = conv0_ref[0]

    u = _rms(h_ref[0], g_ref[...]).astype(BF16)
    z = jnp.dot(u, w_ref[:, 0:W], preferred_element_type=F32)
    xbuf_scr[8:8 + tc, :] = jnp.dot(u, w_ref[:, W:W + C], preferred_element_type=F32)
    dtx_raw = jnp.dot(u, w_ref[:, W + C:2 * W + C], preferred_element_type=F32)
    dtc_raw = jnp.dot(u, w_ref[:, 2 * W + C:2 * W + C + LANE], preferred_element_type=F32)

    conv = convb_ref[...] + convw_ref[SSD_CONV - 1:SSD_CONV, :] * xbuf_scr[8:8 + tc, :]
    for j in range(SSD_CONV - 1):
        conv = conv + convw_ref[j:j + 1, :] * xbuf_scr[5 + j:5 + j + tc, :]
    conv_out_ref[0] = xbuf_scr[tc + 5:tc + 8, :]
    xbuf_scr[0:8, :] = xbuf_scr[tc:tc + 8, :]

    xbc = _silu(conv)
    xs = xbc[:, 0:W]
    bm = xbc[:, W:W + SSD_GROUPS * SSD_STATE].astype(BF16)
    cm = xbc[:, W + SSD_GROUPS * SSD_STATE:C].astype(BF16)

    dtx = _softplus(dtx_raw + dtbx_ref[...])
    dtc = _softplus(dtc_raw + dtbc_ref[...])
    tri = jnp.where(_lower_tri(tc), 1.0, 0.0).astype(BF16)
    acs_x = _mm_exact_lhs(tri, dtx * -jnp.exp(alogx_ref[...]))
    acs_c = _mm_exact_lhs(tri, dtc * -jnp.exp(alogc_ref[...]))
    acs_ct = acs_c.T
    last = acs_x[tc - 1:tc, :]
    e_acs = jnp.exp(acs_x)
    xd = xs * dtx
    xd_b = xd.astype(BF16)
    xd_end = (xd * jnp.exp(last - acs_x)).astype(BF16)
    chunk_decay = jnp.exp(last)
    causal = _lower_tri(tc)

    ys = []
    for g in range(SSD_GROUPS):
        gs = slice(g * SSD_GROUP_W, (g + 1) * SSD_GROUP_W)
        bg = bm[:, g * SSD_STATE:(g + 1) * SSD_STATE]
        cg = cm[:, g * SSD_STATE:(g + 1) * SSD_STATE]
        cb = _mm_nt(cg, bg)
        st_g = st_scr[:, gs]
        y_off = _mm(cg, st_g) * e_acs[:, gs]
        parts = []
        for e in range(SSD_HEADS // SSD_GROUPS):
            hh = g * (SSD_HEADS // SSD_GROUPS) + e
            seg = acs_c[:, hh:hh + 1] - acs_ct[hh:hh + 1, :]
            lmat = jnp.exp(jnp.where(causal, seg, -jnp.inf))
            parts.append(_mm(cb * lmat, xd_b[:, hh * SSD_HEAD_DIM:(hh + 1) * SSD_HEAD_DIM]))
        ys.append(jnp.concatenate(parts, axis=1) + y_off)
        st_scr[:, gs] = st_g * chunk_decay[:, gs] + _mm_tn(bg, xd_end[:, gs])
    y = jnp.concatenate(ys, axis=1) + xs * dx_ref[...]

    t = y * _silu(z)
    outs = []
    for g in range(SSD_GROUPS):
        tg = t[:, g * SSD_GROUP_W:(g + 1) * SSD_GROUP_W]
        outs.append(tg * lax.rsqrt(jnp.mean(tg * tg, axis=-1, keepdims=True) + NORM_EPS))
    o_ref[0] = (jnp.concatenate(outs, axis=1) * gn_ref[...]).astype(o_ref.dtype)

    @pl.when(c == nc - 1)
    def _fin():
        st_out_ref[0] = st_scr[...].T.reshape(SSD_HEADS, SSD_HEAD_DIM, SSD_STATE)


def _const_spec(shape):
    nd = len(shape)
    return pl.BlockSpec(shape, lambda b, c: (0,) * nd, pipeline_mode=pl.Buffered(1))


def _ssd_call(h, lw, st0, conv0, tc):
    B, L, _ = h.shape
    nc = L // tc
    consts = [lw['norm_mix'], lw['w_ssd'], lw['ssd_conv_w'], lw['ssd_conv_b'], lw['ssd_dtb_x'], lw['ssd_alog_x'],
              lw['ssd_dtb_c'], lw['ssd_alog_c'], lw['ssd_d_x'], lw['ssd_norm']]
    in_specs = ([pl.BlockSpec((1, tc, D_MODEL), lambda b, c: (b, c, 0))]
                + [_const_spec(a.shape) for a in consts]
                + [pl.BlockSpec((1, SSD_HEADS, SSD_HEAD_DIM, SSD_STATE), lambda b, c: (b, 0, 0, 0)),
                   pl.BlockSpec((1, SSD_CONV - 1, SSD_CONV_DIM), lambda b, c: (b, 0, 0))])
    out_specs = [pl.BlockSpec((1, tc, D_MODEL), lambda b, c: (b, c, 0)),
                 pl.BlockSpec((1, SSD_HEADS, SSD_HEAD_DIM, SSD_STATE), lambda b, c: (b, 0, 0, 0)),
                 pl.BlockSpec((1, SSD_CONV - 1, SSD_CONV_DIM), lambda b, c: (b, 0, 0))]
    out_shape = [jax.ShapeDtypeStruct((B, L, D_MODEL), BF16),
                 jax.ShapeDtypeStruct((B, SSD_HEADS, SSD_HEAD_DIM, SSD_STATE), F32),
                 jax.ShapeDtypeStruct((B, SSD_CONV - 1, SSD_CONV_DIM), F32)]
    return pl.pallas_call(
        functools.partial(_ssd_kernel, tc=tc),
        grid=(B, nc), in_specs=in_specs, out_specs=out_specs, out_shape=out_shape,
        scratch_shapes=[pltpu.VMEM((SSD_STATE, D_MODEL), F32), pltpu.VMEM((tc + 8, SSD_CONV_DIM), F32)],
        compiler_params=pltpu.CompilerParams(dimension_semantics=("arbitrary", "arbitrary"),
                                             vmem_limit_bytes=VMEM_LIMIT),
        name="ssd_mixer",
    )(h, *consts, st0, conv0)


def _gla_kernel(h_ref, g_ref, w_ref, w2_ref, gb_ref, gn_ref, s0_ref,
                o_ref, s_out_ref,
                s_scr, o_scr, *, tc, q):
    c = pl.program_id(1)
    nc = pl.num_programs(1)
    KD = GLA_KEY_DIM
    W = D_MODEL

    @pl.when(c == 0)
    def _init():
        for hd in range(GLA_HEADS):
            s_scr[hd] = s0_ref[0, hd].T

    u = _rms(h_ref[0], g_ref[...]).astype(BF16)
    qf = jnp.dot(u, w_ref[:, 0:KD], preferred_element_type=F32) * (GLA_DK ** -0.5)
    kf = jnp.dot(u, w_ref[:, KD:2 * KD], preferred_element_type=F32)
    vf = jnp.dot(u, w_ref[:, 2 * KD:2 * KD + W], preferred_element_type=F32).astype(BF16)
    glr = jnp.dot(u, w_ref[:, 2 * KD + 2 * W:2 * KD + 2 * W + LANE], preferred_element_type=F32)
    gpre = _mm(glr, w2_ref[...]) + gb_ref[...]
    gl = -_softplus(-gpre) / GLA_GATE_NORMALIZER

    rr = lax.broadcasted_iota(jnp.int32, (tc, tc), 0)
    cc = lax.broadcasted_iota(jnp.int32, (tc, tc), 1)
    tri = jnp.where((rr // q == cc // q) & (rr >= cc), 1.0, 0.0).astype(BF16)
    gcs = _mm_exact_lhs(tri, gl)
    causal = _lower_tri(q)

    units = []
    decs = []
    for s in range(tc // q):
        rows = slice(s * q, (s + 1) * q)
        g = gcs[rows]
        last = g[q - 1:q, :]
        kk = kf[rows]
        qe = (qf[rows] * jnp.exp(g)).astype(BF16)
        ke = (kk * jnp.exp(-g)).astype(BF16)
        kend = (kk * jnp.exp(last - g)).astype(BF16)
        decs.append(jnp.exp(last))
        for hd in range(GLA_HEADS):
            ks = slice(hd * GLA_DK, (hd + 1) * GLA_DK)
            vs = slice(hd * GLA_DV, (hd + 1) * GLA_DV)
            units.append(dict(rows=rows, ks=ks, vs=vs, qe=qe[:, ks], ke=ke[:, ks], kend=kend[:, ks],
                              v=vf[rows, vs]))
    amats = [jnp.where(causal, _mm_nt(un['qe'], un['ke']), 0.0) for un in units]
    intra = [_mm(a, un['v']) for a, un in zip(amats, units)]
    chunk_states = [_mm_tn(un['v'], un['kend']) for un in units]
    states = [s_scr[hd] for hd in range(GLA_HEADS)]
    for s in range(tc // q):
        for hd in range(GLA_HEADS):
            un = units[s * GLA_HEADS + hd]
            o_scr[un['rows'], un['vs']] = intra[s * GLA_HEADS + hd] + _mm_nt(un['qe'], states[hd])
        states = [st * decs[s][:, units[hd]['ks']] + chunk_states[s * GLA_HEADS + hd]
                  for hd, st in enumerate(states)]
    for hd in range(GLA_HEADS):
        s_scr[hd] = states[hd]

    gate = _silu(jnp.dot(u, w_ref[:, 2 * KD + W:2 * KD + 2 * W], preferred_element_type=F32))
    o = o_scr[...]
    outs = []
    for hd in range(GLA_HEADS):
        oh = o[:, hd * GLA_DV:(hd + 1) * GLA_DV]
        outs.append(oh * lax.rsqrt(jnp.mean(oh * oh, axis=-1, keepdims=True) + NORM_EPS))
    o_ref[0] = (jnp.concatenate(outs, axis=1) * gn_ref[...] * gate).astype(o_ref.dtype)

    @pl.when(c == nc - 1)
    def _fin():
        for hd in range(GLA_HEADS):
            s_out_ref[0, hd] = s_scr[hd].T


def _gla_call(h, lw, s0, tc, q):
    B, L, _ = h.shape
    nc = L // tc
    consts = [lw['norm_mix'], lw['w_gla'], lw['gla_w2'], lw['gla_gb'], lw['gla_norm']]
    in_specs = ([pl.BlockSpec((1, tc, D_MODEL), lambda b, c: (b, c, 0))]
                + [_const_spec(a.shape) for a in consts]
                + [pl.BlockSpec((1, GLA_HEADS, GLA_DK, GLA_DV), lambda b, c: (b, 0, 0, 0))])
    out_specs = [pl.BlockSpec((1, tc, D_MODEL), lambda b, c: (b, c, 0)),
                 pl.BlockSpec((1, GLA_HEADS, GLA_DK, GLA_DV), lambda b, c: (b, 0, 0, 0))]
    out_shape = [jax.ShapeDtypeStruct((B, L, D_MODEL), BF16),
                 jax.ShapeDtypeStruct((B, GLA_HEADS, GLA_DK, GLA_DV), F32)]
    return pl.pallas_call(
        functools.partial(_gla_kernel, tc=tc, q=q),
        grid=(B, nc), in_specs=in_specs, out_specs=out_specs, out_shape=out_shape,
        scratch_shapes=[pltpu.VMEM((GLA_HEADS, GLA_DV, GLA_DK), F32), pltpu.VMEM((tc, D_MODEL), F32)],
        compiler_params=pltpu.CompilerParams(dimension_semantics=("arbitrary", "arbitrary"),
                                             vmem_limit_bytes=VMEM_LIMIT),
        name="gla_mixer",
    )(h, *consts, s0)


def _solve_unit_lower_all(ns, xs, steps):
    wx = xs[0].shape[1]
    wy = wx + ns[0].shape[1]
    keep = lax.broadcasted_iota(jnp.int32, (ns[0].shape[0], wy), 1) < wx
    ys = [jnp.concatenate([x, n], axis=1) for x, n in zip(xs, ns)]
    ps = list(ns)
    for _ in range(steps):
        pys = [_mm(p, y) for p, y in zip(ps, ys)]
        ys = [py + jnp.where(keep, y, 0.0) for py, y in zip(pys, ys)]
        ps = [py[:, wx:wy] for py in pys]
    return [y[:, 0:wx] for y in ys]


def _rwkv_kernel(h_ref, g_ref, w_ref, mu_ref, w0_ref, w2_ref, a0_ref, a2_ref, kk_ref, ka_ref, rk_ref,
                 lnw_ref, lnb_ref, s0_ref, sh0_ref,
                 o_ref, s_out_ref, sh_out_ref,
                 s_scr, carry_scr, o_scr, *, tc, q):
    c = pl.program_id(1)
    nc = pl.num_programs(1)
    W = D_MODEL
    N = RWKV_HEAD

    @pl.when(c == 0)
    def _init():
        s_scr[...] = s0_ref[0]
        carry_scr[...] = sh0_ref[0]

    u = _rms(h_ref[0], g_ref[...]).astype(BF16)
    rf = jnp.dot(u, w_ref[:, 0:RWKV_SHIFT_COLS], preferred_element_type=F32)
    first = lax.broadcasted_iota(jnp.int32, (tc, 1), 0) == 0
    prev = jnp.where(first, carry_scr[...], pltpu.roll(rf, 1, axis=0))
    carry_scr[...] = rf[tc - 1:tc, :]
    sh_out_ref[0] = rf[tc - 1:tc, :]
    rfm = rf + (prev - rf) * mu_ref[...]
    r7 = rfm[:, 0:W]
    k7 = rfm[:, W:2 * W]
    v7 = rfm[:, 2 * W:3 * W]
    lora = rfm[:, 3 * W:3 * W + 2 * RWKV_LORA]
    w_pre = w0_ref[...] + _mm(jnp.tanh(lora), w2_ref[...])
    a = _sigmoid(a0_ref[...] + _mm(lora, a2_ref[...]))
    kkf = k7 * kk_ref[...]
    kk = kkf / jnp.maximum(jnp.sqrt(_head_sums(kkf * kkf, N)), 1e-12)
    k7 = k7 * (1.0 + (a - 1.0) * ka_ref[...])
    bonus = _head_sums(r7 * k7 * rk_ref[...], N) * v7
    pv = -kk
    qv = kk * a
    lw = _sigmoid(w_pre) * (-float(np.exp(-0.5)))

    rr = lax.broadcasted_iota(jnp.int32, (tc, tc), 0)
    cc = lax.broadcasted_iota(jnp.int32, (tc, tc), 1)
    tri = jnp.where((rr // q == cc // q) & (rr >= cc), 1.0, 0.0).astype(BF16)
    cum = _mm_exact_lhs(tri, lw)
    strict = _lower_tri(q, strict=True)
    row2 = lax.broadcasted_iota(jnp.int32, (q, 2 * q), 0)
    col2 = lax.broadcasted_iota(jnp.int32, (q, 2 * q), 1)
    incl2 = row2 >= jnp.where(col2 >= q, col2 - q, col2)
    strict_r = (col2 >= q) & (row2 > col2 - q)
    steps = max(1, (q - 1).bit_length())

    heads = [slice(hd * N, (hd + 1) * N) for hd in range(RWKV_HEADS)]
    units = []
    decs = []
    for s in range(tc // q):
        rows = slice(s * q, (s + 1) * q)
        cum_s = cum[rows]
        last = cum_s[q - 1:q, :]
        e_inv = jnp.exp(-cum_s)
        e_end = jnp.exp(last - cum_s)
        rt = r7[rows] * jnp.exp(cum_s)
        pt = pv[rows] * jnp.exp(cum_s - lw[rows])
        rt_b, pt_b = rt.astype(BF16), pt.astype(BF16)
        qt = (qv[rows] * e_inv).astype(BF16)
        kt = (k7[rows] * e_inv).astype(BF16)
        qend = (qv[rows] * e_end).astype(BF16)
        kend = (k7[rows] * e_end).astype(BF16)
        vv = v7[rows].astype(BF16)
        decs.append(jnp.exp(last))
        for hs in heads:
            units.append(dict(
                rows=rows, hs=hs, rt=rt[:, hs], pt=pt[:, hs], vh=vv[:, hs],
                pr=jnp.concatenate([pt_b[:, hs], rt_b[:, hs]], axis=0),
                qk=jnp.concatenate([qt[:, hs], kt[:, hs]], axis=0),
                ends=jnp.concatenate([qend[:, hs], kend[:, hs]], axis=0)))
    amats = [_mm_nt(un['pr'], un['qk']) for un in units]
    zvs = [_mm(jnp.where(strict_r, am[0:q], 0.0), jnp.concatenate([un['vh'], un['vh']], axis=0))
           for am, un in zip(amats, units)]
    ns = [jnp.where(strict, am[0:q, 0:q], 0.0) for am in amats]
    sols = _solve_unit_lower_all(ns, [jnp.concatenate([zv, un['pt']], axis=1) for zv, un in zip(zvs, units)],
                                 steps)
    x2s = [jnp.concatenate([sol.astype(BF16),
                            jnp.concatenate([un['vh'], jnp.zeros((q, N), BF16)], axis=1)], axis=0)
           for sol, un in zip(sols, units)]
    ows = [_mm(jnp.where(incl2, am[q:2 * q], 0.0), x2) for am, x2 in zip(amats, x2s)]
    sms = [_mm_tn(x2, un['ends']) for x2, un in zip(x2s, units)]
    states = [s_scr[hd] for hd in range(RWKV_HEADS)]
    for s in range(tc // q):
        for hd, hs in enumerate(heads):
            i = s * RWKV_HEADS + hd
            r_eff = units[i]['rt'] + ows[i][:, N:2 * N]
            o_scr[units[i]['rows'], hs] = ows[i][:, 0:N] + _mm_nt(r_eff, states[hd])
        states = [st * decs[s][:, hs] + _mm(st, sms[s * RWKV_HEADS + hd][N:2 * N]) + sms[s * RWKV_HEADS + hd][0:N]
                  for hd, (st, hs) in enumerate(zip(states, heads))]
    for hd in range(RWKV_HEADS):
        s_scr[hd] = states[hd]

    o7 = o_scr[...]
    mean = _head_sums(o7, N) * (1.0 / N)
    d = o7 - mean
    var = _head_sums(d * d, N) * (1.0 / N)
    ln = d * lax.rsqrt(var + RWKV_LN_EPS) * lnw_ref[...] + lnb_ref[...]
    gate = _silu(jnp.dot(u, w_ref[:, RWKV_SHIFT_COLS:RWKV_SHIFT_COLS + W], preferred_element_type=F32))
    o_ref[0] = ((ln + bonus) * gate).astype(o_ref.dtype)

    @pl.when(c == nc - 1)
    def _fin():
        s_out_ref[0] = s_scr[...]


def _rwkv_call(h, lw, s0, sh0, tc, q):
    B, L, _ = h.shape
    nc = L // tc
    consts = [lw['norm_mix'], lw['w_rwkv'], lw['rwkv_mu'], lw['rwkv_w0'], lw['rwkv_w2'], lw['rwkv_a0'], lw['rwkv_a2'],
              lw['rwkv_k_k'], lw['rwkv_k_a'], lw['rwkv_r_k'], lw['rwkv_ln_w'], lw['rwkv_ln_b']]
    in_specs = ([pl.BlockSpec((1, tc, D_MODEL), lambda b, c: (b, c, 0))]
                + [_const_spec(a.shape) for a in consts]
                + [pl.BlockSpec((1, RWKV_HEADS, RWKV_HEAD, RWKV_HEAD), lambda b, c: (b, 0, 0, 0)),
                   pl.BlockSpec((1, 1, RWKV_SHIFT_COLS), lambda b, c: (b, 0, 0))])
    out_specs = [pl.BlockSpec((1, tc, D_MODEL), lambda b, c: (b, c, 0)),
                 pl.BlockSpec((1, RWKV_HEADS, RWKV_HEAD, RWKV_HEAD), lambda b, c: (b, 0, 0, 0)),
                 pl.BlockSpec((1, 1, RWKV_SHIFT_COLS), lambda b, c: (b, 0, 0))]
    out_shape = [jax.ShapeDtypeStruct((B, L, D_MODEL), BF16),
                 jax.ShapeDtypeStruct((B, RWKV_HEADS, RWKV_HEAD, RWKV_HEAD), F32),
                 jax.ShapeDtypeStruct((B, 1, RWKV_SHIFT_COLS), F32)]
    return pl.pallas_call(
        functools.partial(_rwkv_kernel, tc=tc, q=q),
        grid=(B, nc), in_specs=in_specs, out_specs=out_specs, out_shape=out_shape,
        scratch_shapes=[pltpu.VMEM((RWKV_HEADS, RWKV_HEAD, RWKV_HEAD), F32),
                        pltpu.VMEM((1, RWKV_SHIFT_COLS), F32),
                        pltpu.VMEM((tc, D_MODEL), F32)],
        compiler_params=pltpu.CompilerParams(dimension_semantics=("arbitrary", "arbitrary"),
                                             vmem_limit_bytes=VMEM_LIMIT),
        name="rwkv_mixer",
    )(h, *consts, s0, sh0)


def _merge_kernel(h_ref, os_ref, og_ref, or_ref, mk_ref, mv_ref,
                  g_ref, wm_ref, bm_ref, ps_ref, pg_ref, pr_ref, wo_ref, gx_ref, xq_ref, xo_ref, gf_ref,
                  out_ref, *, final):
    W = D_MODEL
    h = h_ref[0]
    u = _rms(h, g_ref[...]).astype(BF16)
    m = None
    for i, (branch_ref, proj_ref) in enumerate(((os_ref, ps_ref), (og_ref, pg_ref), (or_ref, pr_ref))):
        s = _sigmoid(jnp.dot(u, wm_ref[:, i * W:(i + 1) * W], preferred_element_type=F32)
                     + bm_ref[:, i * W:(i + 1) * W])
        term = s * jnp.dot(branch_ref[0], proj_ref[...], preferred_element_type=F32)
        m = term if m is None else m + term
    h1 = h + _mm(m, wo_ref[...])

    u2 = _rms(h1, gx_ref[...])
    qx = _mm(u2, xq_ref[...])
    outs = []
    for hd in range(XA_HEADS):
        hs = slice(hd * XA_HEAD_DIM, (hd + 1) * XA_HEAD_DIM)
        sc = _mm_nt(qx[:, hs], mk_ref[0, :, hs]) * (XA_HEAD_DIM ** -0.5)
        sc = sc - jnp.max(sc, axis=-1, keepdims=True)
        e = jnp.exp(sc)
        p = e / jnp.sum(e, axis=-1, keepdims=True)
        outs.append(_mm(p, mv_ref[0, :, hs]))
    h2 = h1 + _mm(jnp.concatenate(outs, axis=1), xo_ref[...])
    out_ref[0] = _rms(h2, gf_ref[...]) if final else h2


def _merge_call(h, o_ssd, o_gla, o_rwkv, mem_k, mem_v, lw, norm_final, tm, final):
    B, L, _ = h.shape
    M = mem_k.shape[1]
    consts = [lw['norm_mix'], lw['w_merge'], lw['b_merge'], lw['w_proj_ssd'], lw['w_proj_gla'], lw['w_proj_rwkv'],
              lw['w_out'], lw['norm_xattn'], lw['xa_wq'], lw['xa_wo'], norm_final]
    tok = pl.BlockSpec((1, tm, D_MODEL), lambda b, c: (b, c, 0))
    mem = pl.BlockSpec((1, M, D_MODEL), lambda b, c: (b, 0, 0))
    return pl.pallas_call(
        functools.partial(_merge_kernel, final=final),
        grid=(B, L // tm),
        in_specs=[tok, tok, tok, tok, mem, mem] + [_const_spec(a.shape) for a in consts],
        out_specs=tok,
        out_shape=jax.ShapeDtypeStruct((B, L, D_MODEL), F32),
        compiler_params=pltpu.CompilerParams(dimension_semantics=("arbitrary", "arbitrary"),
                                             vmem_limit_bytes=VMEM_LIMIT),
        name="merge_xattn",
    )(h, o_ssd, o_gla, o_rwkv, mem_k, mem_v, *consts)


def _memkv_kernel(x_ref, g_ref, wk_ref, wv_ref, k_ref, v_ref):
    u = _rms(x_ref[...], g_ref[...]).astype(BF16)
    k_ref[...] = jnp.dot(u, wk_ref[...], preferred_element_type=F32)
    v_ref[...] = jnp.dot(u, wv_ref[...], preferred_element_type=F32)


def _memkv_call(mem, g, wk, wv, tm):
    B, M, _ = mem.shape
    x = mem.reshape(B * M, D_MODEL)
    rows = pl.BlockSpec((tm, D_MODEL), lambda i: (i, 0))
    full = lambda a: pl.BlockSpec(a.shape, lambda i: (0,) * a.ndim)
    k, v = pl.pallas_call(
        _memkv_kernel,
        grid=(B * M // tm,),
        in_specs=[rows, full(g), full(wk), full(wv)],
        out_specs=[rows, rows],
        out_shape=[jax.ShapeDtypeStruct((B * M, D_MODEL), F32)] * 2,
        compiler_params=pltpu.CompilerParams(dimension_semantics=("arbitrary",), vmem_limit_bytes=VMEM_LIMIT),
        name="mem_kv",
    )(x, g, wk, wv)
    return k.reshape(B, M, D_MODEL), v.reshape(B, M, D_MODEL)


def _pad_cols(w, n):
    return jnp.pad(w, ((0, 0), (0, n - w.shape[1])))


def _row(v):
    return v.reshape(1, -1).astype(F32)


def _layer_weights(l, P):
    w_in = P['w_in'][l]
    offs = [0]
    for s in _IN_SPLITS:
        offs.append(offs[-1] + s)
    z, xbc, dt, gq, gk, gv, ggate, glr, rf, rgate, merge = (w_in[:, offs[i]:offs[i + 1]] for i in range(len(_IN_SPLITS)))
    rep = lambda v: jnp.repeat(v, SSD_HEAD_DIM, axis=-1)
    zeros_lora = jnp.zeros((RWKV_LORA, D_MODEL), F32)
    lw = {
        'norm_mix': _row(P['norm_mix'][l]),
        'w_ssd': jnp.concatenate([z, xbc, rep(dt), _pad_cols(dt, LANE)], axis=1).astype(BF16),
        'ssd_conv_w': P['ssd_conv_w'][l].astype(F32),
        'ssd_conv_b': _row(P['ssd_conv_b'][l]),
        'ssd_dtb_x': _row(rep(P['ssd_dt_bias'][l])),
        'ssd_alog_x': _row(rep(P['ssd_A_log'][l])),
        'ssd_dtb_c': _pad_cols(_row(P['ssd_dt_bias'][l]), LANE),
        'ssd_alog_c': _pad_cols(_row(P['ssd_A_log'][l]), LANE),
        'ssd_d_x': _row(rep(P['ssd_D'][l])),
        'ssd_norm': _row(P['ssd_norm'][l]),
        'w_gla': jnp.concatenate([gq, gk, gv, ggate, _pad_cols(glr, LANE)], axis=1).astype(BF16),
        'gla_w2': jnp.pad(P['gla_gk_w2'][l], ((0, LANE - GLA_GATE_RANK), (0, 0))).astype(BF16),
        'gla_gb': _row(P['gla_gk_b'][l]),
        'gla_norm': _row(jnp.tile(P['gla_norm'][l], GLA_HEADS)),
        'w_rwkv': jnp.concatenate([rf, rgate], axis=1).astype(BF16),
        'rwkv_mu': _row(P['rwkv_mu'][l]),
        'rwkv_w0': _row(P['rwkv_w0'][l]),
        'rwkv_w2': jnp.concatenate([P['rwkv_w2'][l], zeros_lora], axis=0).astype(BF16),
        'rwkv_a0': _row(P['rwkv_a0'][l]),
        'rwkv_a2': jnp.concatenate([zeros_lora, P['rwkv_a2'][l]], axis=0).astype(BF16),
        'rwkv_k_k': _row(P['rwkv_k_k'][l]),
        'rwkv_k_a': _row(P['rwkv_k_a'][l]),
        'rwkv_r_k': _row(P['rwkv_r_k'][l]),
        'rwkv_ln_w': _row(P['rwkv_ln_w'][l]),
        'rwkv_ln_b': _row(P['rwkv_ln_b'][l]),
        'w_merge': merge.astype(BF16),
        'b_merge': _row(P['b_merge'][l]),
        'w_proj_ssd': P['w_proj_ssd'][l].astype(BF16),
        'w_proj_gla': P['w_proj_gla'][l].astype(BF16),
        'w_proj_rwkv': P['w_proj_rwkv'][l].astype(BF16),
        'w_out': P['w_out'][l].astype(BF16),
        'norm_xattn': _row(P['norm_xattn'][l]),
        'xa_wq': P['xa_wq'][l].astype(BF16),
        'xa_wo': P['xa_wo'][l].astype(BF16),
    }
    return lw


def _block_rows(L, target):
    return target if L % target == 0 else L


def _trunk(h, mem_k, mem_v, ssd_h, conv_buf, gla_h, rwkv_h, shift_buf, layers, norm_final):
    B, L, _ = h.shape
    tc_ssd = _block_rows(L, SSD_BLOCK_ROWS)
    tc_gla = _block_rows(L, GLA_BLOCK_ROWS)
    tc_rwkv = _block_rows(L, RWKV_BLOCK_ROWS)
    tm = _block_rows(L, MERGE_BLOCK_ROWS)
    new = ([], [], [], [], [])
    for l, lw in enumerate(layers):
        o_ssd, ssd_new, conv_new = _ssd_call(h, lw, ssd_h[l], conv_buf[l], tc_ssd)
        o_gla, gla_new = _gla_call(h, lw, gla_h[l], tc_gla, min(GLA_CHUNK, tc_gla))
        o_rwkv, rwkv_new, shift_new = _rwkv_call(h, lw, rwkv_h[l], shift_buf[l], tc_rwkv, min(RWKV_CHUNK, tc_rwkv))
        h = _merge_call(h, o_ssd, o_gla, o_rwkv, mem_k[l], mem_v[l], lw, norm_final, tm,
                        final=(l == len(layers) - 1))
        for lst, s_ in zip(new, (ssd_new, conv_new, gla_new, rwkv_new, shift_new)):
            lst.append(s_)
    return h, tuple(jnp.stack(lst, axis=0) for lst in new)


def kernel(x_prompt, x_sample, mem_prompt, state_ssd, state_ssd_conv, state_gla, state_rwkv, state_rwkv_shift,
           cache_mem_k, cache_mem_v, norm_mix, w_in, ssd_conv_w, ssd_conv_b, ssd_dt_bias, ssd_A_log, ssd_D,
           ssd_norm, w_proj_ssd, gla_gk_w2, gla_gk_b, gla_norm, w_proj_gla, rwkv_mu, rwkv_w0, rwkv_w2,
           rwkv_a0, rwkv_a2, rwkv_k_k, rwkv_k_a, rwkv_r_k, rwkv_ln_w, rwkv_ln_b, w_proj_rwkv, b_merge, w_out,
           norm_xattn, xa_wq, xa_wo, norm_mem, xa_wk, xa_wv, norm_final):
    P = dict(norm_mix=norm_mix, w_in=w_in, ssd_conv_w=ssd_conv_w, ssd_conv_b=ssd_conv_b, ssd_dt_bias=ssd_dt_bias,
             ssd_A_log=ssd_A_log, ssd_D=ssd_D, ssd_norm=ssd_norm, w_proj_ssd=w_proj_ssd, gla_gk_w2=gla_gk_w2,
             gla_gk_b=gla_gk_b, gla_norm=gla_norm, w_proj_gla=w_proj_gla, rwkv_mu=rwkv_mu, rwkv_w0=rwkv_w0,
             rwkv_w2=rwkv_w2, rwkv_a0=rwkv_a0, rwkv_a2=rwkv_a2, rwkv_k_k=rwkv_k_k, rwkv_k_a=rwkv_k_a,
             rwkv_r_k=rwkv_r_k, rwkv_ln_w=rwkv_ln_w, rwkv_ln_b=rwkv_ln_b, w_proj_rwkv=w_proj_rwkv,
             b_merge=b_merge, w_out=w_out, norm_xattn=norm_xattn, xa_wq=xa_wq, xa_wo=xa_wo)
    depth = w_in.shape[0]
    layers = [_layer_weights(l, P) for l in range(depth)]
    gf = _row(norm_final)

    bp, mem_len = mem_prompt.shape[0], mem_prompt.shape[1]
    kv = [_memkv_call(mem_prompt, _row(norm_mem[l]), xa_wk[l].astype(BF16), xa_wv[l].astype(BF16), 256)
          for l in range(depth)]
    mem_k_p = jnp.stack([t[0] for t in kv], axis=0)
    mem_v_p = jnp.stack([t[1] for t in kv], axis=0)

    zeros = lambda shape: jnp.zeros((depth, bp) + shape, F32)
    y_prompt, (p_ssd, p_conv, p_gla, p_rwkv, p_shift) = _trunk(
        x_prompt, mem_k_p, mem_v_p,
        zeros((SSD_HEADS, SSD_HEAD_DIM, SSD_STATE)), zeros((SSD_CONV - 1, SSD_CONV_DIM)),
        zeros((GLA_HEADS, GLA_DK, GLA_DV)), zeros((RWKV_HEADS, RWKV_HEAD, RWKV_HEAD)),
        zeros((1, RWKV_SHIFT_COLS)), layers, gf)

    bs = x_sample.shape[0]
    y_sample, (s_ssd, s_conv, s_gla, s_rwkv, s_shift) = _trunk(
        x_sample, cache_mem_k.reshape(depth, bs, mem_len, D_MODEL), cache_mem_v.reshape(depth, bs, mem_len, D_MODEL),
        state_ssd, state_ssd_conv, state_gla, state_rwkv, state_rwkv_shift, layers, gf)

    kv_shape = (depth, bp, mem_len, XA_HEADS, XA_HEAD_DIM)
    return (y_prompt, y_sample, p_ssd, p_conv, p_gla, p_rwkv, p_shift,
            mem_k_p.reshape(kv_shape), mem_v_p.reshape(kv_shape),
            s_ssd, s_conv, s_gla, s_rwkv, s_shift)
```

```python
import functools

import jax
import jax.numpy as jnp
import numpy as np
from jax import lax
from jax.experimental import pallas as pl
from jax.experimental.pallas import tpu as pltpu

F32 = jnp.float32
BF16 = jnp.bfloat16

D_MODEL = 1024
NORM_EPS = 1e-5

SSD_HEADS = 16
SSD_HEAD_DIM = 64
SSD_GROUPS = 2
SSD_STATE = 128
SSD_CONV = 4
SSD_CONV_DIM = D_MODEL + 2 * SSD_GROUPS * SSD_STATE
SSD_GROUP_W = D_MODEL // SSD_GROUPS

GLA_HEADS = 4
GLA_DK = 128
GLA_DV = 256
GLA_KEY_DIM = GLA_HEADS * GLA_DK
GLA_GATE_RANK = 16
GLA_GATE_NORMALIZER = 16.0
GLA_CHUNK = 64

RWKV_HEADS = 16
RWKV_HEAD = 64
RWKV_LORA = 64
RWKV_SHIFT_COLS = 3 * D_MODEL + 2 * RWKV_LORA
RWKV_LN_EPS = 64e-5
RWKV_CHUNK = 64

XA_HEADS = 4
XA_HEAD_DIM = 256
N_BRANCHES = 3

LANE = 128
SEG_TILE = 256
VMEM_LIMIT = 56 * 1024 * 1024

SSD_BLOCK_ROWS = 256
GLA_BLOCK_ROWS = 512
RWKV_BLOCK_ROWS = 256
MERGE_BLOCK_ROWS = 512

_IN_SPLITS = (D_MODEL, SSD_CONV_DIM, SSD_HEADS, GLA_KEY_DIM, GLA_KEY_DIM, D_MODEL, D_MODEL,
              GLA_GATE_RANK, RWKV_SHIFT_COLS, D_MODEL, N_BRANCHES * D_MODEL)


def _mm(a, b):
    return jnp.dot(a.astype(BF16), b.astype(BF16), preferred_element_type=F32)


def _mm_nt(a, b):
    return lax.dot_general(a.astype(BF16), b.astype(BF16), (((1,), (1,)), ((), ())),
                           preferred_element_type=F32)


def _mm_tn(a, b):
    return lax.dot_general(a.astype(BF16), b.astype(BF16), (((0,), (0,)), ((), ())),
                           preferred_element_type=F32)


def _split2(x):
    hi = x.astype(BF16)
    lo = (x - hi.astype(F32)).astype(BF16)
    return hi, lo


def _mm_exact_lhs(m_bf16, x):
    hi, lo = _split2(x)
    dot = functools.partial(jnp.dot, preferred_element_type=F32)
    return dot(m_bf16, hi) + dot(m_bf16, lo)


def _mm_exact_rhs(x, m_bf16):
    hi, lo = _split2(x)
    dot = functools.partial(jnp.dot, preferred_element_type=F32)
    return dot(hi, m_bf16) + dot(lo, m_bf16)


def _lower_tri(n, strict=False):
    r = lax.broadcasted_iota(jnp.int32, (n, n), 0)
    c = lax.broadcasted_iota(jnp.int32, (n, n), 1)
    return (r > c) if strict else (r >= c)


def _seg_ones(width, seg):
    r = lax.broadcasted_iota(jnp.int32, (width, width), 0) // seg
    c = lax.broadcasted_iota(jnp.int32, (width, width), 1) // seg
    return jnp.where(r == c, 1.0, 0.0).astype(BF16)


def _head_sums(x, seg):
    ones = _seg_ones(SEG_TILE, seg)
    parts = [_mm_exact_rhs(x[:, j:j + SEG_TILE], ones) for j in range(0, x.shape[1], SEG_TILE)]
    return jnp.concatenate(parts, axis=1)


def _rms(x, g):
    return x * lax.rsqrt(jnp.mean(x * x, axis=-1, keepdims=True) + NORM_EPS) * g


def _softplus(x):
    return jnp.maximum(x, 0.0) + jnp.log1p(jnp.exp(-jnp.abs(x)))


def _sigmoid(x):
    return 1.0 / (1.0 + jnp.exp(-x))


def _silu(x):
    return x * _sigmoid(x)


def _ssd_kernel(h_ref, g_ref, w_ref, convw_ref, convb_ref, dtbc_ref, alogc_ref, expand_ref,
                dx_ref, gn_ref, st0_ref, conv0_ref,
                o_ref, st_out_ref, conv_out_ref,
                st_scr, xbuf_scr, *, tc):
    c = pl.program_id(1)
    nc = pl.num_programs(1)
    C = SSD_CONV_DIM
    W = D_MODEL

    @pl.when(c == 0)
    def _init():
        st_scr[...] = st0_ref[0].reshape(W, SSD_STATE).T
        xbuf_scr[0:8, :] = jnp.zeros((8, C), F32)
        xbuf_scr[5:8, :] = conv0_ref[0]

    u = _rms(h_ref[0], g_ref[...]).astype(BF16)
    z = jnp.dot(u, w_ref[:, 0:W], preferred_element_type=F32)
    xbuf_scr[8:8 + tc, :] = jnp.dot(u, w_ref[:, W:W + C], preferred_element_type=F32)
    dtc_raw = jnp.dot(u, w_ref[:, W + C:W + C + LANE], preferred_element_type=F32)

    conv = convb_ref[...] + convw_ref[SSD_CONV - 1:SSD_CONV, :] * xbuf_scr[8:8 + tc, :]
    for j in range(SSD_CONV - 1):
        conv = conv + convw_ref[j:j + 1, :] * xbuf_scr[5 + j:5 + j + tc, :]
    conv_out_ref[0] = xbuf_scr[tc + 5:tc + 8, :]
    xbuf_scr[0:8, :] = xbuf_scr[tc:tc + 8, :]

    xbc = _silu(conv)
    xs = xbc[:, 0:W]
    bm = xbc[:, W:W + SSD_GROUPS * SSD_STATE].astype(BF16)
    cm = xbc[:, W + SSD_GROUPS * SSD_STATE:C].astype(BF16)

    dtc = _softplus(dtc_raw + dtbc_ref[...])
    tri = jnp.where(_lower_tri(tc), 1.0, 0.0).astype(BF16)
    acs_c = _mm_exact_lhs(tri, dtc * -jnp.exp(alogc_ref[...]))
    dtx = _mm_exact_rhs(dtc, expand_ref[...])
    acs_x = _mm_exact_rhs(acs_c, expand_ref[...])
    acs_ct = acs_c.T
    last = acs_x[tc - 1:tc, :]
    e_acs = jnp.exp(acs_x)
    xd = xs * dtx
    xd_b = xd.astype(BF16)
    xd_end = (xd * jnp.exp(last - acs_x)).astype(BF16)
    chunk_decay = jnp.exp(last)
    causal = _lower_tri(tc)

    ys = []
    for g in range(SSD_GROUPS):
        gs = slice(g * SSD_GROUP_W, (g + 1) * SSD_GROUP_W)
        bg = bm[:, g * SSD_STATE:(g + 1) * SSD_STATE]
        cg = cm[:, g * SSD_STATE:(g + 1) * SSD_STATE]
        cb = _mm_nt(cg, bg)
        st_g = st_scr[:, gs]
        y_off = _mm(cg, st_g) * e_acs[:, gs]
        parts = []
        for e in range(SSD_HEADS // SSD_GROUPS):
            hh = g * (SSD_HEADS // SSD_GROUPS) + e
            seg = acs_c[:, hh:hh + 1] - acs_ct[hh:hh + 1, :]
            lmat = jnp.exp(jnp.where(causal, seg, -jnp.inf))
            parts.append(_mm(cb * lmat, xd_b[:, hh * SSD_HEAD_DIM:(hh + 1) * SSD_HEAD_DIM]))
        ys.append(jnp.concatenate(parts, axis=1) + y_off)
        st_scr[:, gs] = st_g * chunk_decay[:, gs] + _mm_tn(bg, xd_end[:, gs])
    y = jnp.concatenate(ys, axis=1) + xs * dx_ref[...]

    t = y * _silu(z)
    outs = []
    for g in range(SSD_GROUPS):
        tg = t[:, g * SSD_GROUP_W:(g + 1) * SSD_GROUP_W]
        outs.append(tg * lax.rsqrt(jnp.mean(tg * tg, axis=-1, keepdims=True) + NORM_EPS))
    o_ref[0] = (jnp.concatenate(outs, axis=1) * gn_ref[...]).astype(o_ref.dtype)

    @pl.when(c == nc - 1)
    def _fin():
        st_out_ref[0] = st_scr[...].T.reshape(SSD_HEADS, SSD_HEAD_DIM, SSD_STATE)


def _const_spec(shape):
    nd = len(shape)
    return pl.BlockSpec(shape, lambda b, c: (0,) * nd, pipeline_mode=pl.Buffered(1))


def _ssd_call(h, lw, st0, conv0, tc):
    B, L, _ = h.shape
    nc = L // tc
    expand = (lax.broadcasted_iota(jnp.int32, (LANE, D_MODEL), 1) // SSD_HEAD_DIM
              == lax.broadcasted_iota(jnp.int32, (LANE, D_MODEL), 0)).astype(BF16)
    consts = [lw['norm_mix'], lw['w_ssd'], lw['ssd_conv_w'], lw['ssd_conv_b'],
              lw['ssd_dtb_c'], lw['ssd_alog_c'], expand, lw['ssd_d_x'], lw['ssd_norm']]
    in_specs = ([pl.BlockSpec((1, tc, D_MODEL), lambda b, c: (b, c, 0))]
                + [_const_spec(a.shape) for a in consts]
                + [pl.BlockSpec((1, SSD_HEADS, SSD_HEAD_DIM, SSD_STATE), lambda b, c: (b, 0, 0, 0)),
                   pl.BlockSpec((1, SSD_CONV - 1, SSD_CONV_DIM), lambda b, c: (b, 0, 0))])
    out_specs = [pl.BlockSpec((1, tc, D_MODEL), lambda b, c: (b, c, 0)),
                 pl.BlockSpec((1, SSD_HEADS, SSD_HEAD_DIM, SSD_STATE), lambda b, c: (b, 0, 0, 0)),
                 pl.BlockSpec((1, SSD_CONV - 1, SSD_CONV_DIM), lambda b, c: (b, 0, 0))]
    out_shape = [jax.ShapeDtypeStruct((B, L, D_MODEL), BF16),
                 jax.ShapeDtypeStruct((B, SSD_HEADS, SSD_HEAD_DIM, SSD_STATE), F32),
                 jax.ShapeDtypeStruct((B, SSD_CONV - 1, SSD_CONV_DIM), F32)]
    return pl.pallas_call(
        functools.partial(_ssd_kernel, tc=tc),
        grid=(B, nc), in_specs=in_specs, out_specs=out_specs, out_shape=out_shape,
        scratch_shapes=[pltpu.VMEM((SSD_STATE, D_MODEL), F32), pltpu.VMEM((tc + 8, SSD_CONV_DIM), F32)],
        compiler_params=pltpu.CompilerParams(dimension_semantics=("arbitrary", "arbitrary"),
                                             vmem_limit_bytes=VMEM_LIMIT),
        name="ssd_mixer",
    )(h, *consts, st0, conv0)


def _gla_kernel(h_ref, g_ref, w_ref, w2_ref, gb_ref, gn_ref, s0_ref,
                o_ref, s_out_ref,
                s_scr, o_scr, *, tc, q):
    c = pl.program_id(1)
    nc = pl.num_programs(1)
    KD = GLA_KEY_DIM
    W = D_MODEL

    @pl.when(c == 0)
    def _init():
        for hd in range(GLA_HEADS):
            s_scr[hd] = s0_ref[0, hd].T

    u = _rms(h_ref[0], g_ref[...]).astype(BF16)
    qf = jnp.dot(u, w_ref[:, 0:KD], preferred_element_type=F32) * (GLA_DK ** -0.5)
    kf = jnp.dot(u, w_ref[:, KD:2 * KD], preferred_element_type=F32)
    vf = jnp.dot(u, w_ref[:, 2 * KD:2 * KD + W], preferred_element_type=F32).astype(BF16)
    glr = jnp.dot(u, w_ref[:, 2 * KD + 2 * W:2 * KD + 2 * W + LANE], preferred_element_type=F32)
    gpre = _mm(glr, w2_ref[...]) + gb_ref[...]
    gl = -_softplus(-gpre) / GLA_GATE_NORMALIZER

    rr = lax.broadcasted_iota(jnp.int32, (tc, tc), 0)
    cc = lax.broadcasted_iota(jnp.int32, (tc, tc), 1)
    tri = jnp.where((rr // q == cc // q) & (rr >= cc), 1.0, 0.0).astype(BF16)
    gcs = _mm_exact_lhs(tri, gl)
    causal = _lower_tri(q)

    units = []
    decs = []
    for s in range(tc // q):
        rows = slice(s * q, (s + 1) * q)
        g = gcs[rows]
        last = g[q - 1:q, :]
        kk = kf[rows]
        qe = (qf[rows] * jnp.exp(g)).astype(BF16)
        ke = (kk * jnp.exp(-g)).astype(BF16)
        kend = (kk * jnp.exp(last - g)).astype(BF16)
        decs.append(jnp.exp(last))
        for hd in range(GLA_HEADS):
            ks = slice(hd * GLA_DK, (hd + 1) * GLA_DK)
            vs = slice(hd * GLA_DV, (hd + 1) * GLA_DV)
            units.append(dict(rows=rows, ks=ks, vs=vs, qe=qe[:, ks], ke=ke[:, ks], kend=kend[:, ks],
                              v=vf[rows, vs]))
    amats = [jnp.where(causal, _mm_nt(un['qe'], un['ke']), 0.0) for un in units]
    intra = [_mm(a, un['v']) for a, un in zip(amats, units)]
    chunk_states = [_mm_tn(un['v'], un['kend']) for un in units]
    states = [s_scr[hd] for hd in range(GLA_HEADS)]
    for s in range(tc // q):
        for hd in range(GLA_HEADS):
            un = units[s * GLA_HEADS + hd]
            o_scr[un['rows'], un['vs']] = intra[s * GLA_HEADS + hd] + _mm_nt(un['qe'], states[hd])
        states = [st * decs[s][:, units[hd]['ks']] + chunk_states[s * GLA_HEADS + hd]
                  for hd, st in enumerate(states)]
    for hd in range(GLA_HEADS):
        s_scr[hd] = states[hd]

    gate = _silu(jnp.dot(u, w_ref[:, 2 * KD + W:2 * KD + 2 * W], preferred_element_type=F32))
    o = o_scr[...]
    outs = []
    for hd in range(GLA_HEADS):
        oh = o[:, hd * GLA_DV:(hd + 1) * GLA_DV]
        outs.append(oh * lax.rsqrt(jnp.mean(oh * oh, axis=-1, keepdims=True) + NORM_EPS))
    o_ref[0] = (jnp.concatenate(outs, axis=1) * gn_ref[...] * gate).astype(o_ref.dtype)

    @pl.when(c == nc - 1)
    def _fin():
        for hd in range(GLA_HEADS):
            s_out_ref[0, hd] = s_scr[hd].T


def _gla_call(h, lw, s0, tc, q):
    B, L, _ = h.shape
    nc = L // tc
    consts = [lw['norm_mix'], lw['w_gla'], lw['gla_w2'], lw['gla_gb'], lw['gla_norm']]
    in_specs = ([pl.BlockSpec((1, tc, D_MODEL), lambda b, c: (b, c, 0))]
                + [_const_spec(a.shape) for a in consts]
                + [pl.BlockSpec((1, GLA_HEADS, GLA_DK, GLA_DV), lambda b, c: (b, 0, 0, 0))])
    out_specs = [pl.BlockSpec((1, tc, D_MODEL), lambda b, c: (b, c, 0)),
                 pl.BlockSpec((1, GLA_HEADS, GLA_DK, GLA_DV), lambda b, c: (b, 0, 0, 0))]
    out_shape = [jax.ShapeDtypeStruct((B, L, D_MODEL), BF16),
                 jax.ShapeDtypeStruct((B, GLA_HEADS, GLA_DK, GLA_DV), F32)]
    return pl.pallas_call(
        functools.partial(_gla_kernel, tc=tc, q=q),
        grid=(B, nc), in_specs=in_specs, out_specs=out_specs, out_shape=out_shape,
        scratch_shapes=[pltpu.VMEM((GLA_HEADS, GLA_DV, GLA_DK), F32), pltpu.VMEM((tc, D_MODEL), F32)],
        compiler_params=pltpu.CompilerParams(dimension_semantics=("arbitrary", "arbitrary"),
                                             vmem_limit_bytes=VMEM_LIMIT),
        name="gla_mixer",
    )(h, *consts, s0)


def _rwkv_kernel(h_ref, g_ref, w_ref, mu_ref, w0_ref, w2_ref, a0_ref, a2_ref, kk_ref, ka_ref, rk_ref,
                 lnw_ref, lnb_ref, s0_ref, sh0_ref,
                 o_ref, s_out_ref, sh_out_ref,
                 s_scr, carry_scr, o_scr, *, tc, q):
    c = pl.program_id(1)
    nc = pl.num_programs(1)
    W = D_MODEL
    N = RWKV_HEAD

    @pl.when(c == 0)
    def _init():
        for hd in range(RWKV_HEADS):
            s_scr[:, hd * N:(hd + 1) * N] = s0_ref[0, hd]
        carry_scr[...] = sh0_ref[0]

    u = _rms(h_ref[0], g_ref[...]).astype(BF16)
    rf = jnp.dot(u, w_ref[:, 0:RWKV_SHIFT_COLS], preferred_element_type=F32)
    first = lax.broadcasted_iota(jnp.int32, (tc, 1), 0) == 0
    prev = jnp.where(first, carry_scr[...], pltpu.roll(rf, 1, axis=0))
    carry_scr[...] = rf[tc - 1:tc, :]
    sh_out_ref[0] = rf[tc - 1:tc, :]
    rfm = rf + (prev - rf) * mu_ref[...]
    r7 = rfm[:, 0:W]
    k7 = rfm[:, W:2 * W]
    v7 = rfm[:, 2 * W:3 * W]
    lora = rfm[:, 3 * W:3 * W + 2 * RWKV_LORA]
    w_pre = w0_ref[...] + _mm(jnp.tanh(lora), w2_ref[...])
    a = _sigmoid(a0_ref[...] + _mm(lora, a2_ref[...]))
    kkf = k7 * kk_ref[...]
    kk = kkf / jnp.maximum(jnp.sqrt(_head_sums(kkf * kkf, N)), 1e-12)
    k7 = k7 * (1.0 + (a - 1.0) * ka_ref[...])
    bonus = _head_sums(r7 * k7 * rk_ref[...], N) * v7
    pv = -kk
    qv = kk * a
    lw = _sigmoid(w_pre) * (-float(np.exp(-0.5)))

    rr = lax.broadcasted_iota(jnp.int32, (tc, tc), 0)
    cc = lax.broadcasted_iota(jnp.int32, (tc, tc), 1)
    tri = jnp.where((rr // q == cc // q) & (rr >= cc), 1.0, 0.0).astype(BF16)
    cum = _mm_exact_lhs(tri, lw)
    steps = max(1, (q - 1).bit_length())
    PW = 2 * N

    def left(x):
        return jnp.where(lax.broadcasted_iota(jnp.int32, x.shape, 1) < N, x, jnp.zeros_like(x))

    def right(x):
        return jnp.where(lax.broadcasted_iota(jnp.int32, x.shape, 1) >= N, x, jnp.zeros_like(x))

    def pick(a_src, b_src):
        return jnp.where(lax.broadcasted_iota(jnp.int32, a_src.shape, 1) < a_src.shape[1] // 2, a_src, b_src)

    row4 = lax.broadcasted_iota(jnp.int32, (q, 4 * q), 0)
    col4 = lax.broadcasted_iota(jnp.int32, (q, 4 * q), 1)
    blk4 = col4 // q
    tcol4 = col4 - blk4 * q
    outer4 = (blk4 == 0) | (blk4 == 3)
    incl4 = row4 >= tcol4
    strict_k4 = (~outer4) & (row4 > tcol4)
    strict_q4 = outer4 & (row4 > tcol4)
    eye_x4 = (~outer4) & (row4 == tcol4)

    npairs = RWKV_HEADS // 2
    tiles = [slice(t * PW, (t + 1) * PW) for t in range(npairs)]
    zq = jnp.zeros((q, PW), BF16)
    units = []
    decs = []
    for s in range(tc // q):
        rows = slice(s * q, (s + 1) * q)
        cum_s = cum[rows]
        last = cum_s[q - 1:q, :]
        e_inv = jnp.exp(-cum_s)
        e_end = jnp.exp(last - cum_s)
        rt = r7[rows] * jnp.exp(cum_s)
        pt = pv[rows] * jnp.exp(cum_s - lw[rows])
        rt_b, pt_b = rt.astype(BF16), pt.astype(BF16)
        qt = (qv[rows] * e_inv).astype(BF16)
        kt = (k7[rows] * e_inv).astype(BF16)
        qend = (qv[rows] * e_end).astype(BF16)
        kend = (k7[rows] * e_end).astype(BF16)
        vv = v7[rows].astype(BF16)
        decs.append(jnp.exp(last))
        for ts in tiles:
            vt = vv[:, ts]
            units.append(dict(
                rows=rows, ts=ts, rt=rt[:, ts], pt=pt[:, ts],
                pr=jnp.concatenate([pt_b[:, ts], rt_b[:, ts]], axis=0),
                qk4=jnp.concatenate([left(qt[:, ts]), left(kt[:, ts]), right(kt[:, ts]), right(qt[:, ts])], axis=0),
                ends4=jnp.concatenate([left(qend[:, ts]), left(kend[:, ts]),
                                       right(kend[:, ts]), right(qend[:, ts])], axis=0),
                va=left(vt), vb=right(vt)))
    amats = [_mm_nt(un['pr'], un['qk4']) for un in units]
    zvs = [_mm(jnp.where(strict_k4, am[0:q], 0.0), jnp.concatenate([un['va'], un['va'], un['vb'], un['vb']], axis=0))
           for am, un in zip(amats, units)]
    ys = [jnp.where(strict_q4, am[0:q], 0.0) + jnp.where(eye_x4, 1.0, 0.0) for am in amats]
    zh = jnp.zeros((q, 2 * q), BF16)
    for _ in range(steps):
        pys = []
        for y in ys:
            t0, t1 = y[:, 0:2 * q].astype(BF16), y[:, 2 * q:4 * q].astype(BF16)
            rhs = jnp.concatenate([jnp.concatenate([t0, zh], axis=1), jnp.concatenate([zh, t1], axis=1)], axis=0)
            pys.append(_mm(pick(t0, t1), rhs))
        ys = [py + jnp.where(outer4, 0.0, y) for py, y in zip(pys, ys)]
    sols = []
    for y, zv, un in zip(ys, zvs, units):
        pt_sw = pltpu.roll(un['pt'], N, axis=1)
        r_a = pick(zv, pt_sw).astype(BF16)
        r_b = pick(pt_sw, zv).astype(BF16)
        rhs = jnp.concatenate([jnp.concatenate([zq, r_b], axis=1), jnp.concatenate([r_a, zq], axis=1)], axis=0)
        sols.append(_mm(pick(y[:, 2 * q:4 * q], y[:, 0:2 * q]), rhs).astype(BF16))
    x4s = [jnp.concatenate([jnp.concatenate([sol[:, 0:PW], zq], axis=1),
                            jnp.concatenate([un['va'], zq], axis=1),
                            jnp.concatenate([zq, un['vb']], axis=1),
                            jnp.concatenate([zq, sol[:, PW:2 * PW]], axis=1)], axis=0)
           for sol, un in zip(sols, units)]
    ows = [_mm(jnp.where(incl4, am[q:2 * q], 0.0), x4) for am, x4 in zip(amats, x4s)]
    sms = [_mm_tn(x4, un['ends4']) for x4, un in zip(x4s, units)]
    states = [s_scr[:, ts] for ts in tiles]
    for s in range(tc // q):
        for t, ts in enumerate(tiles):
            i = s * npairs + t
            ow0, ow1 = ows[i][:, 0:PW], ows[i][:, PW:2 * PW]
            r_eff = units[i]['rt'] + pltpu.roll(pick(ow1, ow0), N, axis=1)
            bd = jnp.concatenate([left(states[t]), right(states[t])], axis=0)
            o_scr[units[i]['rows'], ts] = pick(ow0, ow1) + _mm_nt(r_eff, bd)
        nxt = []
        for t, ts in enumerate(tiles):
            sm = sms[s * npairs + t]
            m_t = sm[N:2 * N] + sm[2 * N:3 * N]
            mbd = jnp.concatenate([left(m_t), right(m_t)], axis=0)
            nxt.append(states[t] * decs[s][:, ts] + _mm(states[t], mbd) + (sm[0:N] + sm[3 * N:4 * N]))
        states = nxt
    for t, ts in enumerate(tiles):
        s_scr[:, ts] = states[t]

    o7 = o_scr[...]
    mean = _head_sums(o7, N) * (1.0 / N)
    d = o7 - mean
    var = _head_sums(d * d, N) * (1.0 / N)
    ln = d * lax.rsqrt(var + RWKV_LN_EPS) * lnw_ref[...] + lnb_ref[...]
    gate = _silu(jnp.dot(u, w_ref[:, RWKV_SHIFT_COLS:RWKV_SHIFT_COLS + W], preferred_element_type=F32))
    o_ref[0] = ((ln + bonus) * gate).astype(o_ref.dtype)

    @pl.when(c == nc - 1)
    def _fin():
        for hd in range(RWKV_HEADS):
            s_out_ref[0, hd] = s_scr[:, hd * N:(hd + 1) * N]


def _rwkv_call(h, lw, s0, sh0, tc, q):
    B, L, _ = h.shape
    nc = L // tc
    consts = [lw['norm_mix'], lw['w_rwkv'], lw['rwkv_mu'], lw['rwkv_w0'], lw['rwkv_w2'], lw['rwkv_a0'], lw['rwkv_a2'],
              lw['rwkv_k_k'], lw['rwkv_k_a'], lw['rwkv_r_k'], lw['rwkv_ln_w'], lw['rwkv_ln_b']]
    in_specs = ([pl.BlockSpec((1, tc, D_MODEL), lambda b, c: (b, c, 0))]
                + [_const_spec(a.shape) for a in consts]
                + [pl.BlockSpec((1, RWKV_HEADS, RWKV_HEAD, RWKV_HEAD), lambda b, c: (b, 0, 0, 0)),
                   pl.BlockSpec((1, 1, RWKV_SHIFT_COLS), lambda b, c: (b, 0, 0))])
    out_specs = [pl.BlockSpec((1, tc, D_MODEL), lambda b, c: (b, c, 0)),
                 pl.BlockSpec((1, RWKV_HEADS, RWKV_HEAD, RWKV_HEAD), lambda b, c: (b, 0, 0, 0)),
                 pl.BlockSpec((1, 1, RWKV_SHIFT_COLS), lambda b, c: (b, 0, 0))]
    out_shape = [jax.ShapeDtypeStruct((B, L, D_MODEL), BF16),
                 jax.ShapeDtypeStruct((B, RWKV_HEADS, RWKV_HEAD, RWKV_HEAD), F32),
                 jax.ShapeDtypeStruct((B, 1, RWKV_SHIFT_COLS), F32)]
    return pl.pallas_call(
        functools.partial(_rwkv_kernel, tc=tc, q=q),
        grid=(B, nc), in_specs=in_specs, out_specs=out_specs, out_shape=out_shape,
        scratch_shapes=[pltpu.VMEM((RWKV_HEAD, D_MODEL), F32),
                        pltpu.VMEM((1, RWKV_SHIFT_COLS), F32),
                        pltpu.VMEM((tc, D_MODEL), F32)],
        compiler_params=pltpu.CompilerParams(dimension_semantics=("arbitrary", "arbitrary"),
                                             vmem_limit_bytes=VMEM_LIMIT),
        name="rwkv_mixer",
    )(h, *consts, s0, sh0)


def _merge_kernel(h_ref, os_ref, og_ref, or_ref, mk_ref, mv_ref,
                  g_ref, wm_ref, bm_ref, ps_ref, pg_ref, pr_ref, wo_ref, gx_ref, xq_ref, xo_ref, gf_ref,
                  out_ref, k_scr, v_scr, *, final):
    W = D_MODEL

    @pl.when(pl.program_id(1) == 0)
    def _stage_memory():
        for hd in range(XA_HEADS):
            k_scr[hd] = mk_ref[0, 0, :, hd, :].astype(BF16)
            v_scr[hd] = mv_ref[0, 0, :, hd, :].astype(BF16)

    h = h_ref[0]
    u = _rms(h, g_ref[...]).astype(BF16)
    m = None
    for i, (branch_ref, proj_ref) in enumerate(((os_ref, ps_ref), (og_ref, pg_ref), (or_ref, pr_ref))):
        s = _sigmoid(jnp.dot(u, wm_ref[:, i * W:(i + 1) * W], preferred_element_type=F32)
                     + bm_ref[:, i * W:(i + 1) * W])
        term = s * jnp.dot(branch_ref[0], proj_ref[...], preferred_element_type=F32)
        m = term if m is None else m + term
    h1 = h + _mm(m, wo_ref[...])

    u2 = _rms(h1, gx_ref[...])
    qx = _mm(u2, xq_ref[...])
    outs = []
    for hd in range(XA_HEADS):
        hs = slice(hd * XA_HEAD_DIM, (hd + 1) * XA_HEAD_DIM)
        sc = _mm_nt(qx[:, hs], k_scr[hd]) * (XA_HEAD_DIM ** -0.5)
        sc = sc - jnp.max(sc, axis=-1, keepdims=True)
        e = jnp.exp(sc)
        p = e / jnp.sum(e, axis=-1, keepdims=True)
        outs.append(_mm(p, v_scr[hd]))
    h2 = h1 + _mm(jnp.concatenate(outs, axis=1), xo_ref[...])
    out_ref[0] = _rms(h2, gf_ref[...]) if final else h2


def _merge_call(h, o_ssd, o_gla, o_rwkv, mem_k, mem_v, layer, lw, norm_final, tm, final):
    B, L, _ = h.shape
    M = mem_k.shape[2]
    consts = [lw['norm_mix'], lw['w_merge'], lw['b_merge'], lw['w_proj_ssd'], lw['w_proj_gla'], lw['w_proj_rwkv'],
              lw['w_out'], lw['norm_xattn'], lw['xa_wq'], lw['xa_wo'], norm_final]
    tok = pl.BlockSpec((1, tm, D_MODEL), lambda b, c: (b, c, 0))
    mem = pl.BlockSpec((1, 1, M, XA_HEADS, XA_HEAD_DIM), lambda b, c: (layer, b, 0, 0, 0))
    return pl.pallas_call(
        functools.partial(_merge_kernel, final=final),
        grid=(B, L // tm),
        in_specs=[tok, tok, tok, tok, mem, mem] + [_const_spec(a.shape) for a in consts],
        out_specs=tok,
        out_shape=jax.ShapeDtypeStruct((B, L, D_MODEL), F32),
        scratch_shapes=[pltpu.VMEM((XA_HEADS, M, XA_HEAD_DIM), BF16), pltpu.VMEM((XA_HEADS, M, XA_HEAD_DIM), BF16)],
        compiler_params=pltpu.CompilerParams(dimension_semantics=("arbitrary", "arbitrary"),
                                             vmem_limit_bytes=VMEM_LIMIT),
        name="merge_xattn",
    )(h, o_ssd, o_gla, o_rwkv, mem_k, mem_v, *consts)


def _memkv_kernel(x_ref, g_ref, wk_ref, wv_ref, k_ref, v_ref):
    u = _rms(x_ref[...], g_ref[0]).astype(BF16)
    k = jnp.dot(u, wk_ref[0], preferred_element_type=F32)
    v = jnp.dot(u, wv_ref[0], preferred_element_type=F32)
    for hd in range(XA_HEADS):
        hs = slice(hd * XA_HEAD_DIM, (hd + 1) * XA_HEAD_DIM)
        k_ref[0, :, hd, :] = k[:, hs]
        v_ref[0, :, hd, :] = v[:, hs]


def _memkv_call(mem, g, wk, wv, tm):
    B, M, _ = mem.shape
    depth = wk.shape[0]
    x = mem.reshape(B * M, D_MODEL)
    rows = pl.BlockSpec((tm, D_MODEL), lambda l, i: (i, 0))
    per_layer = lambda a: pl.BlockSpec((1,) + a.shape[1:], lambda l, i: (l,) + (0,) * (a.ndim - 1))
    out = pl.BlockSpec((1, tm, XA_HEADS, XA_HEAD_DIM), lambda l, i: (l, i, 0, 0))
    k, v = pl.pallas_call(
        _memkv_kernel,
        grid=(depth, B * M // tm),
        in_specs=[rows, per_layer(g), per_layer(wk), per_layer(wv)],
        out_specs=[out, out],
        out_shape=[jax.ShapeDtypeStruct((depth, B * M, XA_HEADS, XA_HEAD_DIM), F32)] * 2,
        compiler_params=pltpu.CompilerParams(dimension_semantics=("arbitrary", "arbitrary"),
                                             vmem_limit_bytes=VMEM_LIMIT),
        name="mem_kv",
    )(x, g, wk, wv)
    shape = (depth, B, M, XA_HEADS, XA_HEAD_DIM)
    return k.reshape(shape), v.reshape(shape)


def _pad_cols(w, n):
    return jnp.pad(w, ((0, 0), (0, n - w.shape[1])))


def _row(v):
    return v.reshape(1, -1).astype(F32)


def _layer_weights(l, P):
    w_in = P['w_in'][l]
    offs = [0]
    for s in _IN_SPLITS:
        offs.append(offs[-1] + s)
    z, xbc, dt, gq, gk, gv, ggate, glr, rf, rgate, merge = (w_in[:, offs[i]:offs[i + 1]] for i in range(len(_IN_SPLITS)))
    rep = lambda v: jnp.repeat(v, SSD_HEAD_DIM, axis=-1)
    zeros_lora = jnp.zeros((RWKV_LORA, D_MODEL), F32)
    lw = {
        'norm_mix': _row(P['norm_mix'][l]),
        'w_ssd': jnp.concatenate([z, xbc, _pad_cols(dt, LANE)], axis=1).astype(BF16),
        'ssd_conv_w': P['ssd_conv_w'][l].astype(F32),
        'ssd_conv_b': _row(P['ssd_conv_b'][l]),
        'ssd_dtb_c': _pad_cols(_row(P['ssd_dt_bias'][l]), LANE),
        'ssd_alog_c': _pad_cols(_row(P['ssd_A_log'][l]), LANE),
        'ssd_d_x': _row(rep(P['ssd_D'][l])),
        'ssd_norm': _row(P['ssd_norm'][l]),
        'w_gla': jnp.concatenate([gq, gk, gv, ggate, _pad_cols(glr, LANE)], axis=1).astype(BF16),
        'gla_w2': jnp.pad(P['gla_gk_w2'][l], ((0, LANE - GLA_GATE_RANK), (0, 0))).astype(BF16),
        'gla_gb': _row(P['gla_gk_b'][l]),
        'gla_norm': _row(jnp.tile(P['gla_norm'][l], GLA_HEADS)),
        'w_rwkv': jnp.concatenate([rf, rgate], axis=1).astype(BF16),
        'rwkv_mu': _row(P['rwkv_mu'][l]),
        'rwkv_w0': _row(P['rwkv_w0'][l]),
        'rwkv_w2': jnp.concatenate([P['rwkv_w2'][l], zeros_lora], axis=0).astype(BF16),
        'rwkv_a0': _row(P['rwkv_a0'][l]),
        'rwkv_a2': jnp.concatenate([zeros_lora, P['rwkv_a2'][l]], axis=0).astype(BF16),
        'rwkv_k_k': _row(P['rwkv_k_k'][l]),
        'rwkv_k_a': _row(P['rwkv_k_a'][l]),
        'rwkv_r_k': _row(P['rwkv_r_k'][l]),
        'rwkv_ln_w': _row(P['rwkv_ln_w'][l]),
        'rwkv_ln_b': _row(P['rwkv_ln_b'][l]),
        'w_merge': merge.astype(BF16),
        'b_merge': _row(P['b_merge'][l]),
        'w_proj_ssd': P['w_proj_ssd'][l].astype(BF16),
        'w_proj_gla': P['w_proj_gla'][l].astype(BF16),
        'w_proj_rwkv': P['w_proj_rwkv'][l].astype(BF16),
        'w_out': P['w_out'][l].astype(BF16),
        'norm_xattn': _row(P['norm_xattn'][l]),
        'xa_wq': P['xa_wq'][l].astype(BF16),
        'xa_wo': P['xa_wo'][l].astype(BF16),
    }
    return lw


def _block_rows(L, target):
    return target if L % target == 0 else L


def _trunk(h, mem_k, mem_v, ssd_h, conv_buf, gla_h, rwkv_h, shift_buf, layers, norm_final):
    B, L, _ = h.shape
    tc_ssd = _block_rows(L, SSD_BLOCK_ROWS)
    tc_gla = _block_rows(L, GLA_BLOCK_ROWS)
    tc_rwkv = _block_rows(L, RWKV_BLOCK_ROWS)
    tm = _block_rows(L, MERGE_BLOCK_ROWS)
    new = ([], [], [], [], [])
    for l, lw in enumerate(layers):
        o_ssd, ssd_new, conv_new = _ssd_call(h, lw, ssd_h[l], conv_buf[l], tc_ssd)
        o_gla, gla_new = _gla_call(h, lw, gla_h[l], tc_gla, min(GLA_CHUNK, tc_gla))
        o_rwkv, rwkv_new, shift_new = _rwkv_call(h, lw, rwkv_h[l], shift_buf[l], tc_rwkv, min(RWKV_CHUNK, tc_rwkv))
        h = _merge_call(h, o_ssd, o_gla, o_rwkv, mem_k, mem_v, l, lw, norm_final, tm,
                        final=(l == len(layers) - 1))
        for lst, s_ in zip(new, (ssd_new, conv_new, gla_new, rwkv_new, shift_new)):
            lst.append(s_)
    return h, tuple(jnp.stack(lst, axis=0) for lst in new)


def kernel(x_prompt, x_sample, mem_prompt, state_ssd, state_ssd_conv, state_gla, state_rwkv, state_rwkv_shift,
           cache_mem_k, cache_mem_v, norm_mix, w_in, ssd_conv_w, ssd_conv_b, ssd_dt_bias, ssd_A_log, ssd_D,
           ssd_norm, w_proj_ssd, gla_gk_w2, gla_gk_b, gla_norm, w_proj_gla, rwkv_mu, rwkv_w0, rwkv_w2,
           rwkv_a0, rwkv_a2, rwkv_k_k, rwkv_k_a, rwkv_r_k, rwkv_ln_w, rwkv_ln_b, w_proj_rwkv, b_merge, w_out,
           norm_xattn, xa_wq, xa_wo, norm_mem, xa_wk, xa_wv, norm_final):
    P = dict(norm_mix=norm_mix, w_in=w_in, ssd_conv_w=ssd_conv_w, ssd_conv_b=ssd_conv_b, ssd_dt_bias=ssd_dt_bias,
             ssd_A_log=ssd_A_log, ssd_D=ssd_D, ssd_norm=ssd_norm, w_proj_ssd=w_proj_ssd, gla_gk_w2=gla_gk_w2,
             gla_gk_b=gla_gk_b, gla_norm=gla_norm, w_proj_gla=w_proj_gla, rwkv_mu=rwkv_mu, rwkv_w0=rwkv_w0,
             rwkv_w2=rwkv_w2, rwkv_a0=rwkv_a0, rwkv_a2=rwkv_a2, rwkv_k_k=rwkv_k_k, rwkv_k_a=rwkv_k_a,
             rwkv_r_k=rwkv_r_k, rwkv_ln_w=rwkv_ln_w, rwkv_ln_b=rwkv_ln_b, w_proj_rwkv=w_proj_rwkv,
             b_merge=b_merge, w_out=w_out, norm_xattn=norm_xattn, xa_wq=xa_wq, xa_wo=xa_wo)
    depth = w_in.shape[0]
    layers = [_layer_weights(l, P) for l in range(depth)]
    gf = _row(norm_final)

    bp = mem_prompt.shape[0]
    mem_k_p, mem_v_p = _memkv_call(mem_prompt, norm_mem.reshape(depth, 1, D_MODEL).astype(F32),
                                   xa_wk.astype(BF16), xa_wv.astype(BF16), 256)

    zeros = lambda shape: jnp.zeros((depth, bp) + shape, F32)
    y_prompt, (p_ssd, p_conv, p_gla, p_rwkv, p_shift) = _trunk(
        x_prompt, mem_k_p, mem_v_p,
        zeros((SSD_HEADS, SSD_HEAD_DIM, SSD_STATE)), zeros((SSD_CONV - 1, SSD_CONV_DIM)),
        zeros((GLA_HEADS, GLA_DK, GLA_DV)), zeros((RWKV_HEADS, RWKV_HEAD, RWKV_HEAD)),
        zeros((1, RWKV_SHIFT_COLS)), layers, gf)

    y_sample, (s_ssd, s_conv, s_gla, s_rwkv, s_shift) = _trunk(
        x_sample, cache_mem_k, cache_mem_v,
        state_ssd, state_ssd_conv, state_gla, state_rwkv, state_rwkv_shift, layers, gf)

    return (y_prompt, y_sample, p_ssd, p_conv, p_gla, p_rwkv, p_shift, mem_k_p, mem_v_p,
            s_ssd, s_conv, s_gla, s_rwkv, s_shift)
```

```python
import functools

import jax
import jax.numpy as jnp
import numpy as np
from jax import lax
from jax.experimental import pallas as pl
from jax.experimental.pallas import tpu as pltpu

F32 = jnp.float32
BF16 = jnp.bfloat16

D_MODEL = 1024
NORM_EPS = 1e-5

SSD_HEADS = 16
SSD_HEAD_DIM = 64
SSD_GROUPS = 2
SSD_STATE = 128
SSD_CONV = 4
SSD_CONV_DIM = D_MODEL + 2 * SSD_GROUPS * SSD_STATE
SSD_GROUP_W = D_MODEL // SSD_GROUPS

GLA_HEADS = 4
GLA_DK = 128
GLA_DV = 256
GLA_KEY_DIM = GLA_HEADS * GLA_DK
GLA_GATE_RANK = 16
GLA_GATE_NORMALIZER = 16.0
GLA_CHUNK = 64
GLA_GROUP_CHUNKS = 8

RWKV_HEADS = 16
RWKV_HEAD = 64
RWKV_LORA = 64
RWKV_SHIFT_COLS = 3 * D_MODEL + 2 * RWKV_LORA
RWKV_LN_EPS = 64e-5
RWKV_CHUNK = 64
RWKV_GROUP_CHUNKS = 2

XA_HEADS = 4
XA_HEAD_DIM = 256
N_BRANCHES = 3

LANE = 128
SEG_TILE = 256
VMEM_LIMIT = 56 * 1024 * 1024

SSD_BLOCK_ROWS = 256
GLA_BLOCK_ROWS = 512
RWKV_BLOCK_ROWS = 512
MERGE_BLOCK_ROWS = 512

_IN_SPLITS = (D_MODEL, SSD_CONV_DIM, SSD_HEADS, GLA_KEY_DIM, GLA_KEY_DIM, D_MODEL, D_MODEL,
              GLA_GATE_RANK, RWKV_SHIFT_COLS, D_MODEL, N_BRANCHES * D_MODEL)


def _mm(a, b):
    return jnp.dot(a.astype(BF16), b.astype(BF16), preferred_element_type=F32)


def _mm_nt(a, b):
    return lax.dot_general(a.astype(BF16), b.astype(BF16), (((1,), (1,)), ((), ())),
                           preferred_element_type=F32)


def _mm_tn(a, b):
    return lax.dot_general(a.astype(BF16), b.astype(BF16), (((0,), (0,)), ((), ())),
                           preferred_element_type=F32)


def _split2(x):
    hi = x.astype(BF16)
    lo = (x - hi.astype(F32)).astype(BF16)
    return hi, lo


def _mm_exact_lhs(m_bf16, x):
    hi, lo = _split2(x)
    dot = functools.partial(jnp.dot, preferred_element_type=F32)
    return dot(m_bf16, hi) + dot(m_bf16, lo)


def _mm_exact_rhs(x, m_bf16):
    hi, lo = _split2(x)
    dot = functools.partial(jnp.dot, preferred_element_type=F32)
    return dot(hi, m_bf16) + dot(lo, m_bf16)


def _lower_tri(n, strict=False):
    r = lax.broadcasted_iota(jnp.int32, (n, n), 0)
    c = lax.broadcasted_iota(jnp.int32, (n, n), 1)
    return (r > c) if strict else (r >= c)


def _seg_ones(width, seg):
    r = lax.broadcasted_iota(jnp.int32, (width, width), 0) // seg
    c = lax.broadcasted_iota(jnp.int32, (width, width), 1) // seg
    return jnp.where(r == c, 1.0, 0.0).astype(BF16)


def _head_sums(x, seg):
    ones = _seg_ones(SEG_TILE, seg)
    parts = [_mm_exact_rhs(x[:, j:j + SEG_TILE], ones) for j in range(0, x.shape[1], SEG_TILE)]
    return jnp.concatenate(parts, axis=1)


def _rms(x, g):
    return x * lax.rsqrt(jnp.mean(x * x, axis=-1, keepdims=True) + NORM_EPS) * g


def _softplus(x):
    return jnp.maximum(x, 0.0) + jnp.log1p(jnp.exp(-jnp.abs(x)))


def _sigmoid(x):
    return 1.0 / (1.0 + jnp.exp(-x))


def _silu(x):
    return x * _sigmoid(x)


def _ssd_kernel(h_ref, g_ref, w_ref, convw_ref, convb_ref, dtbc_ref, alogc_ref, expand_ref,
                dx_ref, gn_ref, st0_ref, conv0_ref,
                o_ref, st_out_ref, conv_out_ref,
                st_scr, xbuf_scr, *, tc):
    c = pl.program_id(1)
    nc = pl.num_programs(1)
    C = SSD_CONV_DIM
    W = D_MODEL

    @pl.when(c == 0)
    def _init():
        st_scr[...] = st0_ref[0].reshape(W, SSD_STATE).T
        xbuf_scr[0:8, :] = jnp.zeros((8, C), F32)
        xbuf_scr[5:8, :] = conv0_ref[0]

    u = _rms(h_ref[0], g_ref[...]).astype(BF16)
    z = jnp.dot(u, w_ref[:, 0:W], preferred_element_type=F32)
    xbuf_scr[8:8 + tc, :] = jnp.dot(u, w_ref[:, W:W + C], preferred_element_type=F32)
    dtc_raw = jnp.dot(u, w_ref[:, W + C:W + C + LANE], preferred_element_type=F32)

    conv = convb_ref[...] + convw_ref[SSD_CONV - 1:SSD_CONV, :] * xbuf_scr[8:8 + tc, :]
    for j in range(SSD_CONV - 1):
        conv = conv + convw_ref[j:j + 1, :] * xbuf_scr[5 + j:5 + j + tc, :]
    conv_out_ref[0] = xbuf_scr[tc + 5:tc + 8, :]
    xbuf_scr[0:8, :] = xbuf_scr[tc:tc + 8, :]

    xbc = _silu(conv)
    xs = xbc[:, 0:W]
    bm = xbc[:, W:W + SSD_GROUPS * SSD_STATE].astype(BF16)
    cm = xbc[:, W + SSD_GROUPS * SSD_STATE:C].astype(BF16)

    dtc = _softplus(dtc_raw + dtbc_ref[...])
    tri = jnp.where(_lower_tri(tc), 1.0, 0.0).astype(BF16)
    acs_c = _mm_exact_lhs(tri, dtc * -jnp.exp(alogc_ref[...]))
    dtx = _mm_exact_rhs(dtc, expand_ref[...])
    acs_x = _mm_exact_rhs(acs_c, expand_ref[...])
    acs_ct = acs_c.T
    last = acs_x[tc - 1:tc, :]
    e_acs = jnp.exp(acs_x)
    xd = xs * dtx
    xd_b = xd.astype(BF16)
    xd_end = (xd * jnp.exp(last - acs_x)).astype(BF16)
    chunk_decay = jnp.exp(last)
    causal = _lower_tri(tc)

    ys = []
    for g in range(SSD_GROUPS):
        gs = slice(g * SSD_GROUP_W, (g + 1) * SSD_GROUP_W)
        bg = bm[:, g * SSD_STATE:(g + 1) * SSD_STATE]
        cg = cm[:, g * SSD_STATE:(g + 1) * SSD_STATE]
        cb = _mm_nt(cg, bg)
        st_g = st_scr[:, gs]
        y_off = _mm(cg, st_g) * e_acs[:, gs]
        parts = []
        for e in range(SSD_HEADS // SSD_GROUPS):
            hh = g * (SSD_HEADS // SSD_GROUPS) + e
            seg = acs_c[:, hh:hh + 1] - acs_ct[hh:hh + 1, :]
            lmat = jnp.exp(jnp.where(causal, seg, -jnp.inf))
            parts.append(_mm(cb * lmat, xd_b[:, hh * SSD_HEAD_DIM:(hh + 1) * SSD_HEAD_DIM]))
        ys.append(jnp.concatenate(parts, axis=1) + y_off)
        st_scr[:, gs] = st_g * chunk_decay[:, gs] + _mm_tn(bg, xd_end[:, gs])
    y = jnp.concatenate(ys, axis=1) + xs * dx_ref[...]

    t = y * _silu(z)
    outs = []
    for g in range(SSD_GROUPS):
        tg = t[:, g * SSD_GROUP_W:(g + 1) * SSD_GROUP_W]
        outs.append(tg * lax.rsqrt(jnp.mean(tg * tg, axis=-1, keepdims=True) + NORM_EPS))
    o_ref[0] = (jnp.concatenate(outs, axis=1) * gn_ref[...]).astype(o_ref.dtype)

    @pl.when(c == nc - 1)
    def _fin():
        st_out_ref[0] = st_scr[...].T.reshape(SSD_HEADS, SSD_HEAD_DIM, SSD_STATE)


def _const_spec(shape):
    nd = len(shape)
    return pl.BlockSpec(shape, lambda b, c: (0,) * nd, pipeline_mode=pl.Buffered(1))


def _ssd_call(h, lw, st0, conv0, tc):
    B, L, _ = h.shape
    nc = L // tc
    expand = (lax.broadcasted_iota(jnp.int32, (LANE, D_MODEL), 1) // SSD_HEAD_DIM
              == lax.broadcasted_iota(jnp.int32, (LANE, D_MODEL), 0)).astype(BF16)
    consts = [lw['norm_mix'], lw['w_ssd'], lw['ssd_conv_w'], lw['ssd_conv_b'],
              lw['ssd_dtb_c'], lw['ssd_alog_c'], expand, lw['ssd_d_x'], lw['ssd_norm']]
    in_specs = ([pl.BlockSpec((1, tc, D_MODEL), lambda b, c: (b, c, 0))]
                + [_const_spec(a.shape) for a in consts]
                + [pl.BlockSpec((1, SSD_HEADS, SSD_HEAD_DIM, SSD_STATE), lambda b, c: (b, 0, 0, 0)),
                   pl.BlockSpec((1, SSD_CONV - 1, SSD_CONV_DIM), lambda b, c: (b, 0, 0))])
    out_specs = [pl.BlockSpec((1, tc, D_MODEL), lambda b, c: (b, c, 0)),
                 pl.BlockSpec((1, SSD_HEADS, SSD_HEAD_DIM, SSD_STATE), lambda b, c: (b, 0, 0, 0)),
                 pl.BlockSpec((1, SSD_CONV - 1, SSD_CONV_DIM), lambda b, c: (b, 0, 0))]
    out_shape = [jax.ShapeDtypeStruct((B, L, D_MODEL), BF16),
                 jax.ShapeDtypeStruct((B, SSD_HEADS, SSD_HEAD_DIM, SSD_STATE), F32),
                 jax.ShapeDtypeStruct((B, SSD_CONV - 1, SSD_CONV_DIM), F32)]
    return pl.pallas_call(
        functools.partial(_ssd_kernel, tc=tc),
        grid=(B, nc), in_specs=in_specs, out_specs=out_specs, out_shape=out_shape,
        scratch_shapes=[pltpu.VMEM((SSD_STATE, D_MODEL), F32), pltpu.VMEM((tc + 8, SSD_CONV_DIM), F32)],
        compiler_params=pltpu.CompilerParams(dimension_semantics=("arbitrary", "arbitrary"),
                                             vmem_limit_bytes=VMEM_LIMIT),
        name="ssd_mixer",
    )(h, *consts, st0, conv0)


def _gla_kernel(h_ref, g_ref, w_ref, w2_ref, gb_ref, gn_ref, s0_ref,
                o_ref, s_out_ref,
                s_scr, o_scr, *, tc, q):
    c = pl.program_id(1)
    nc = pl.num_programs(1)
    KD = GLA_KEY_DIM
    W = D_MODEL

    @pl.when(c == 0)
    def _init():
        for hd in range(GLA_HEADS):
            s_scr[hd] = s0_ref[0, hd].T

    u = _rms(h_ref[0], g_ref[...]).astype(BF16)
    qf = jnp.dot(u, w_ref[:, 0:KD], preferred_element_type=F32) * (GLA_DK ** -0.5)
    kf = jnp.dot(u, w_ref[:, KD:2 * KD], preferred_element_type=F32)
    vf = jnp.dot(u, w_ref[:, 2 * KD:2 * KD + W], preferred_element_type=F32).astype(BF16)
    glr = jnp.dot(u, w_ref[:, 2 * KD + 2 * W:2 * KD + 2 * W + LANE], preferred_element_type=F32)
    gpre = _mm(glr, w2_ref[...]) + gb_ref[...]
    gl = -_softplus(-gpre) / GLA_GATE_NORMALIZER

    rr = lax.broadcasted_iota(jnp.int32, (tc, tc), 0)
    cc = lax.broadcasted_iota(jnp.int32, (tc, tc), 1)
    tri = jnp.where((rr // q == cc // q) & (rr >= cc), 1.0, 0.0).astype(BF16)
    gcs = _mm_exact_lhs(tri, gl)
    causal = _lower_tri(q)

    def chunk_units(s):
        rows = slice(s * q, (s + 1) * q)
        g = gcs[rows]
        last = g[q - 1:q, :]
        kk = kf[rows]
        qe = (qf[rows] * jnp.exp(g)).astype(BF16)
        ke = (kk * jnp.exp(-g)).astype(BF16)
        kend = (kk * jnp.exp(last - g)).astype(BF16)
        dec = jnp.exp(last)
        out = []
        for hd in range(GLA_HEADS):
            ks = slice(hd * GLA_DK, (hd + 1) * GLA_DK)
            vs = slice(hd * GLA_DV, (hd + 1) * GLA_DV)
            out.append(dict(rows=rows, vs=vs, qe=qe[:, ks], ke=ke[:, ks], kend=kend[:, ks], dec=dec[:, ks],
                            v=vf[rows, vs]))
        return out

    states = [s_scr[hd] for hd in range(GLA_HEADS)]
    for s0 in range(0, tc // q, GLA_GROUP_CHUNKS):
        group = [un for s in range(s0, min(s0 + GLA_GROUP_CHUNKS, tc // q)) for un in chunk_units(s)]
        amats = [jnp.where(causal, _mm_nt(un['qe'], un['ke']), 0.0) for un in group]
        intra = [_mm(a, un['v']) for a, un in zip(amats, group)]
        chunk_states = [_mm_tn(un['v'], un['kend']) for un in group]
        for j in range(0, len(group), GLA_HEADS):
            for hd in range(GLA_HEADS):
                un = group[j + hd]
                o_scr[un['rows'], un['vs']] = intra[j + hd] + _mm_nt(un['qe'], states[hd])
            states = [st * group[j + hd]['dec'] + chunk_states[j + hd] for hd, st in enumerate(states)]
    for hd in range(GLA_HEADS):
        s_scr[hd] = states[hd]

    gate = _silu(jnp.dot(u, w_ref[:, 2 * KD + W:2 * KD + 2 * W], preferred_element_type=F32))
    o = o_scr[...]
    outs = []
    for hd in range(GLA_HEADS):
        oh = o[:, hd * GLA_DV:(hd + 1) * GLA_DV]
        outs.append(oh * lax.rsqrt(jnp.mean(oh * oh, axis=-1, keepdims=True) + NORM_EPS))
    o_ref[0] = (jnp.concatenate(outs, axis=1) * gn_ref[...] * gate).astype(o_ref.dtype)

    @pl.when(c == nc - 1)
    def _fin():
        for hd in range(GLA_HEADS):
            s_out_ref[0, hd] = s_scr[hd].T


def _gla_call(h, lw, s0, tc, q):
    B, L, _ = h.shape
    nc = L // tc
    consts = [lw['norm_mix'], lw['w_gla'], lw['gla_w2'], lw['gla_gb'], lw['gla_norm']]
    in_specs = ([pl.BlockSpec((1, tc, D_MODEL), lambda b, c: (b, c, 0))]
                + [_const_spec(a.shape) for a in consts]
                + [pl.BlockSpec((1, GLA_HEADS, GLA_DK, GLA_DV), lambda b, c: (b, 0, 0, 0))])
    out_specs = [pl.BlockSpec((1, tc, D_MODEL), lambda b, c: (b, c, 0)),
                 pl.BlockSpec((1, GLA_HEADS, GLA_DK, GLA_DV), lambda b, c: (b, 0, 0, 0))]
    out_shape = [jax.ShapeDtypeStruct((B, L, D_MODEL), BF16),
                 jax.ShapeDtypeStruct((B, GLA_HEADS, GLA_DK, GLA_DV), F32)]
    return pl.pallas_call(
        functools.partial(_gla_kernel, tc=tc, q=q),
        grid=(B, nc), in_specs=in_specs, out_specs=out_specs, out_shape=out_shape,
        scratch_shapes=[pltpu.VMEM((GLA_HEADS, GLA_DV, GLA_DK), F32), pltpu.VMEM((tc, D_MODEL), F32)],
        compiler_params=pltpu.CompilerParams(dimension_semantics=("arbitrary", "arbitrary"),
                                             vmem_limit_bytes=VMEM_LIMIT),
        name="gla_mixer",
    )(h, *consts, s0)


def _rwkv_kernel(h_ref, g_ref, w_ref, mu_ref, w0_ref, w2_ref, a0_ref, a2_ref, kk_ref, ka_ref, rk_ref,
                 lnw_ref, lnb_ref, s0_ref, sh0_ref,
                 o_ref, s_out_ref, sh_out_ref,
                 s_scr, carry_scr, o_scr, *, tc, q):
    c = pl.program_id(1)
    nc = pl.num_programs(1)
    W = D_MODEL
    N = RWKV_HEAD

    @pl.when(c == 0)
    def _init():
        for hd in range(RWKV_HEADS):
            s_scr[:, hd * N:(hd + 1) * N] = s0_ref[0, hd]
        carry_scr[...] = sh0_ref[0]

    u = _rms(h_ref[0], g_ref[...]).astype(BF16)
    rf = jnp.dot(u, w_ref[:, 0:RWKV_SHIFT_COLS], preferred_element_type=F32)
    first = lax.broadcasted_iota(jnp.int32, (tc, 1), 0) == 0
    prev = jnp.where(first, carry_scr[...], pltpu.roll(rf, 1, axis=0))
    carry_scr[...] = rf[tc - 1:tc, :]
    sh_out_ref[0] = rf[tc - 1:tc, :]
    rfm = rf + (prev - rf) * mu_ref[...]
    r7 = rfm[:, 0:W]
    k7 = rfm[:, W:2 * W]
    v7 = rfm[:, 2 * W:3 * W]
    lora = rfm[:, 3 * W:3 * W + 2 * RWKV_LORA]
    w_pre = w0_ref[...] + _mm(jnp.tanh(lora), w2_ref[...])
    a = _sigmoid(a0_ref[...] + _mm(lora, a2_ref[...]))
    kkf = k7 * kk_ref[...]
    kk = kkf / jnp.maximum(jnp.sqrt(_head_sums(kkf * kkf, N)), 1e-12)
    k7 = k7 * (1.0 + (a - 1.0) * ka_ref[...])
    bonus = _head_sums(r7 * k7 * rk_ref[...], N) * v7
    pv = -kk
    qv = kk * a
    lw = _sigmoid(w_pre) * (-float(np.exp(-0.5)))

    rr = lax.broadcasted_iota(jnp.int32, (tc, tc), 0)
    cc = lax.broadcasted_iota(jnp.int32, (tc, tc), 1)
    tri = jnp.where((rr // q == cc // q) & (rr >= cc), 1.0, 0.0).astype(BF16)
    cum = _mm_exact_lhs(tri, lw)
    steps = max(1, (q - 1).bit_length())
    PW = 2 * N

    def left(x):
        return jnp.where(lax.broadcasted_iota(jnp.int32, x.shape, 1) < N, x, jnp.zeros_like(x))

    def right(x):
        return jnp.where(lax.broadcasted_iota(jnp.int32, x.shape, 1) >= N, x, jnp.zeros_like(x))

    def pick(a_src, b_src):
        return jnp.where(lax.broadcasted_iota(jnp.int32, a_src.shape, 1) < a_src.shape[1] // 2, a_src, b_src)

    row4 = lax.broadcasted_iota(jnp.int32, (q, 4 * q), 0)
    col4 = lax.broadcasted_iota(jnp.int32, (q, 4 * q), 1)
    blk4 = col4 // q
    tcol4 = col4 - blk4 * q
    outer4 = (blk4 == 0) | (blk4 == 3)
    incl4 = row4 >= tcol4
    strict_k4 = (~outer4) & (row4 > tcol4)
    strict_q4 = outer4 & (row4 > tcol4)
    eye_x4 = (~outer4) & (row4 == tcol4)

    npairs = RWKV_HEADS // 2
    tiles = [slice(t * PW, (t + 1) * PW) for t in range(npairs)]
    zq = jnp.zeros((q, PW), BF16)
    decs = []

    def chunk_units(s):
        rows = slice(s * q, (s + 1) * q)
        cum_s = cum[rows]
        last = cum_s[q - 1:q, :]
        e_inv = jnp.exp(-cum_s)
        e_end = jnp.exp(last - cum_s)
        rt = r7[rows] * jnp.exp(cum_s)
        pt = pv[rows] * jnp.exp(cum_s - lw[rows])
        rt_b, pt_b = rt.astype(BF16), pt.astype(BF16)
        qt = (qv[rows] * e_inv).astype(BF16)
        kt = (k7[rows] * e_inv).astype(BF16)
        qend = (qv[rows] * e_end).astype(BF16)
        kend = (k7[rows] * e_end).astype(BF16)
        vv = v7[rows].astype(BF16)
        decs.append(jnp.exp(last))
        out = []
        for ts in tiles:
            vt = vv[:, ts]
            out.append(dict(
                rows=rows, ts=ts, rt=rt[:, ts], pt=pt[:, ts],
                pr=jnp.concatenate([pt_b[:, ts], rt_b[:, ts]], axis=0),
                qk4=jnp.concatenate([left(qt[:, ts]), left(kt[:, ts]), right(kt[:, ts]), right(qt[:, ts])], axis=0),
                ends4=jnp.concatenate([left(qend[:, ts]), left(kend[:, ts]),
                                       right(kend[:, ts]), right(qend[:, ts])], axis=0),
                va=left(vt), vb=right(vt)))
        return out
    zh = jnp.zeros((q, 2 * q), BF16)

    def local_terms(group):
        amats = [_mm_nt(un['pr'], un['qk4']) for un in group]
        zvs = [_mm(jnp.where(strict_k4, am[0:q], 0.0),
                   jnp.concatenate([un['va'], un['va'], un['vb'], un['vb']], axis=0)) for am, un in zip(amats, group)]
        ys = [jnp.where(strict_q4, am[0:q], 0.0) + jnp.where(eye_x4, 1.0, 0.0) for am in amats]
        for _ in range(steps):
            pys = []
            for y in ys:
                t0, t1 = y[:, 0:2 * q].astype(BF16), y[:, 2 * q:4 * q].astype(BF16)
                rhs = jnp.concatenate([jnp.concatenate([t0, zh], axis=1), jnp.concatenate([zh, t1], axis=1)], axis=0)
                pys.append(_mm(pick(t0, t1), rhs))
            ys = [py + jnp.where(outer4, 0.0, y) for py, y in zip(pys, ys)]
        sols = []
        for y, zv, un in zip(ys, zvs, group):
            pt_sw = pltpu.roll(un['pt'], N, axis=1)
            r_a = pick(zv, pt_sw).astype(BF16)
            r_b = pick(pt_sw, zv).astype(BF16)
            rhs = jnp.concatenate([jnp.concatenate([zq, r_b], axis=1), jnp.concatenate([r_a, zq], axis=1)], axis=0)
            sols.append(_mm(pick(y[:, 2 * q:4 * q], y[:, 0:2 * q]), rhs).astype(BF16))
        x4s = [jnp.concatenate([jnp.concatenate([sol[:, 0:PW], zq], axis=1),
                                jnp.concatenate([un['va'], zq], axis=1),
                                jnp.concatenate([zq, un['vb']], axis=1),
                                jnp.concatenate([zq, sol[:, PW:2 * PW]], axis=1)], axis=0)
               for sol, un in zip(sols, group)]
        return ([_mm(jnp.where(incl4, am[q:2 * q], 0.0), x4) for am, x4 in zip(amats, x4s)],
                [_mm_tn(x4, un['ends4']) for x4, un in zip(x4s, group)])

    def advance(states, s, group, ows, sms):
        for t, ts in enumerate(tiles):
            ow0, ow1 = ows[t][:, 0:PW], ows[t][:, PW:2 * PW]
            r_eff = group[t]['rt'] + pltpu.roll(pick(ow1, ow0), N, axis=1)
            bd = jnp.concatenate([left(states[t]), right(states[t])], axis=0)
            o_scr[group[t]['rows'], ts] = pick(ow0, ow1) + _mm_nt(r_eff, bd)
        nxt = []
        for t, ts in enumerate(tiles):
            sm = sms[t]
            m_t = sm[N:2 * N] + sm[2 * N:3 * N]
            mbd = jnp.concatenate([left(m_t), right(m_t)], axis=0)
            nxt.append(states[t] * decs[s][:, ts] + _mm(states[t], mbd) + (sm[0:N] + sm[3 * N:4 * N]))
        return nxt

    states = [s_scr[:, ts] for ts in tiles]
    for s0 in range(0, tc // q, RWKV_GROUP_CHUNKS):
        chunks = range(s0, min(s0 + RWKV_GROUP_CHUNKS, tc // q))
        group = [un for s in chunks for un in chunk_units(s)]
        o_g, s_g = local_terms(group)
        for j, s in enumerate(chunks):
            sl = slice(j * npairs, (j + 1) * npairs)
            states = advance(states, s, group[sl], o_g[sl], s_g[sl])
    for t, ts in enumerate(tiles):
        s_scr[:, ts] = states[t]

    o7 = o_scr[...]
    mean = _head_sums(o7, N) * (1.0 / N)
    d = o7 - mean
    var = _head_sums(d * d, N) * (1.0 / N)
    ln = d * lax.rsqrt(var + RWKV_LN_EPS) * lnw_ref[...] + lnb_ref[...]
    gate = _silu(jnp.dot(u, w_ref[:, RWKV_SHIFT_COLS:RWKV_SHIFT_COLS + W], preferred_element_type=F32))
    o_ref[0] = ((ln + bonus) * gate).astype(o_ref.dtype)

    @pl.when(c == nc - 1)
    def _fin():
        for hd in range(RWKV_HEADS):
            s_out_ref[0, hd] = s_scr[:, hd * N:(hd + 1) * N]


def _rwkv_call(h, lw, s0, sh0, tc, q):
    B, L, _ = h.shape
    nc = L // tc
    consts = [lw['norm_mix'], lw['w_rwkv'], lw['rwkv_mu'], lw['rwkv_w0'], lw['rwkv_w2'], lw['rwkv_a0'], lw['rwkv_a2'],
              lw['rwkv_k_k'], lw['rwkv_k_a'], lw['rwkv_r_k'], lw['rwkv_ln_w'], lw['rwkv_ln_b']]
    in_specs = ([pl.BlockSpec((1, tc, D_MODEL), lambda b, c: (b, c, 0))]
                + [_const_spec(a.shape) for a in consts]
                + [pl.BlockSpec((1, RWKV_HEADS, RWKV_HEAD, RWKV_HEAD), lambda b, c: (b, 0, 0, 0)),
                   pl.BlockSpec((1, 1, RWKV_SHIFT_COLS), lambda b, c: (b, 0, 0))])
    out_specs = [pl.BlockSpec((1, tc, D_MODEL), lambda b, c: (b, c, 0)),
                 pl.BlockSpec((1, RWKV_HEADS, RWKV_HEAD, RWKV_HEAD), lambda b, c: (b, 0, 0, 0)),
                 pl.BlockSpec((1, 1, RWKV_SHIFT_COLS), lambda b, c: (b, 0, 0))]
    out_shape = [jax.ShapeDtypeStruct((B, L, D_MODEL), BF16),
                 jax.ShapeDtypeStruct((B, RWKV_HEADS, RWKV_HEAD, RWKV_HEAD), F32),
                 jax.ShapeDtypeStruct((B, 1, RWKV_SHIFT_COLS), F32)]
    return pl.pallas_call(
        functools.partial(_rwkv_kernel, tc=tc, q=q),
        grid=(B, nc), in_specs=in_specs, out_specs=out_specs, out_shape=out_shape,
        scratch_shapes=[pltpu.VMEM((RWKV_HEAD, D_MODEL), F32),
                        pltpu.VMEM((1, RWKV_SHIFT_COLS), F32),
                        pltpu.VMEM((tc, D_MODEL), F32)],
        compiler_params=pltpu.CompilerParams(dimension_semantics=("arbitrary", "arbitrary"),
                                             vmem_limit_bytes=VMEM_LIMIT),
        name="rwkv_mixer",
    )(h, *consts, s0, sh0)


def _merge_kernel(h_ref, os_ref, og_ref, or_ref, mk_ref, mv_ref,
                  g_ref, wm_ref, bm_ref, ps_ref, pg_ref, pr_ref, wo_ref, gx_ref, xq_ref, xo_ref, gf_ref,
                  out_ref, k_scr, v_scr, *, final):
    W = D_MODEL

    @pl.when(pl.program_id(1) == 0)
    def _stage_memory():
        for hd in range(XA_HEADS):
            k_scr[hd] = mk_ref[0, 0, :, hd, :].astype(BF16)
            v_scr[hd] = mv_ref[0, 0, :, hd, :].astype(BF16)

    h = h_ref[0]
    u = _rms(h, g_ref[...]).astype(BF16)
    m = None
    for i, (branch_ref, proj_ref) in enumerate(((os_ref, ps_ref), (og_ref, pg_ref), (or_ref, pr_ref))):
        s = _sigmoid(jnp.dot(u, wm_ref[:, i * W:(i + 1) * W], preferred_element_type=F32)
                     + bm_ref[:, i * W:(i + 1) * W])
        term = s * jnp.dot(branch_ref[0], proj_ref[...], preferred_element_type=F32)
        m = term if m is None else m + term
    h1 = h + _mm(m, wo_ref[...])

    u2 = _rms(h1, gx_ref[...])
    qx = _mm(u2, xq_ref[...])
    outs = []
    for hd in range(XA_HEADS):
        hs = slice(hd * XA_HEAD_DIM, (hd + 1) * XA_HEAD_DIM)
        sc = _mm_nt(qx[:, hs], k_scr[hd]) * (XA_HEAD_DIM ** -0.5)
        sc = sc - jnp.max(sc, axis=-1, keepdims=True)
        e = jnp.exp(sc)
        p = e / jnp.sum(e, axis=-1, keepdims=True)
        outs.append(_mm(p, v_scr[hd]))
    h2 = h1 + _mm(jnp.concatenate(outs, axis=1), xo_ref[...])
    out_ref[0] = _rms(h2, gf_ref[...]) if final else h2


def _merge_call(h, o_ssd, o_gla, o_rwkv, mem_k, mem_v, layer, lw, norm_final, tm, final):
    B, L, _ = h.shape
    M = mem_k.shape[2]
    consts = [lw['norm_mix'], lw['w_merge'], lw['b_merge'], lw['w_proj_ssd'], lw['w_proj_gla'], lw['w_proj_rwkv'],
              lw['w_out'], lw['norm_xattn'], lw['xa_wq'], lw['xa_wo'], norm_final]
    tok = pl.BlockSpec((1, tm, D_MODEL), lambda b, c: (b, c, 0))
    mem = pl.BlockSpec((1, 1, M, XA_HEADS, XA_HEAD_DIM), lambda b, c: (layer, b, 0, 0, 0))
    return pl.pallas_call(
        functools.partial(_merge_kernel, final=final),
        grid=(B, L // tm),
        in_specs=[tok, tok, tok, tok, mem, mem] + [_const_spec(a.shape) for a in consts],
        out_specs=tok,
        out_shape=jax.ShapeDtypeStruct((B, L, D_MODEL), F32),
        scratch_shapes=[pltpu.VMEM((XA_HEADS, M, XA_HEAD_DIM), BF16), pltpu.VMEM((XA_HEADS, M, XA_HEAD_DIM), BF16)],
        compiler_params=pltpu.CompilerParams(dimension_semantics=("arbitrary", "arbitrary"),
                                             vmem_limit_bytes=VMEM_LIMIT),
        name="merge_xattn",
    )(h, o_ssd, o_gla, o_rwkv, mem_k, mem_v, *consts)


def _memkv_kernel(x_ref, g_ref, wk_ref, wv_ref, k_ref, v_ref):
    u = _rms(x_ref[...], g_ref[0]).astype(BF16)
    k = jnp.dot(u, wk_ref[0], preferred_element_type=F32)
    v = jnp.dot(u, wv_ref[0], preferred_element_type=F32)
    for hd in range(XA_HEADS):
        hs = slice(hd * XA_HEAD_DIM, (hd + 1) * XA_HEAD_DIM)
        k_ref[0, :, hd, :] = k[:, hs]
        v_ref[0, :, hd, :] = v[:, hs]


def _memkv_call(mem, g, wk, wv, tm):
    B, M, _ = mem.shape
    depth = wk.shape[0]
    x = mem.reshape(B * M, D_MODEL)
    rows = pl.BlockSpec((tm, D_MODEL), lambda l, i: (i, 0))
    per_layer = lambda a: pl.BlockSpec((1,) + a.shape[1:], lambda l, i: (l,) + (0,) * (a.ndim - 1))
    out = pl.BlockSpec((1, tm, XA_HEADS, XA_HEAD_DIM), lambda l, i: (l, i, 0, 0))
    k, v = pl.pallas_call(
        _memkv_kernel,
        grid=(depth, B * M // tm),
        in_specs=[rows, per_layer(g), per_layer(wk), per_layer(wv)],
        out_specs=[out, out],
        out_shape=[jax.ShapeDtypeStruct((depth, B * M, XA_HEADS, XA_HEAD_DIM), F32)] * 2,
        compiler_params=pltpu.CompilerParams(dimension_semantics=("arbitrary", "arbitrary"),
                                             vmem_limit_bytes=VMEM_LIMIT),
        name="mem_kv",
    )(x, g, wk, wv)
    shape = (depth, B, M, XA_HEADS, XA_HEAD_DIM)
    return k.reshape(shape), v.reshape(shape)


def _pad_cols(w, n):
    return jnp.pad(w, ((0, 0), (0, n - w.shape[1])))


def _row(v):
    return v.reshape(1, -1).astype(F32)


def _layer_weights(l, P):
    w_in = P['w_in'][l]
    offs = [0]
    for s in _IN_SPLITS:
        offs.append(offs[-1] + s)
    z, xbc, dt, gq, gk, gv, ggate, glr, rf, rgate, merge = (w_in[:, offs[i]:offs[i + 1]] for i in range(len(_IN_SPLITS)))
    rep = lambda v: jnp.repeat(v, SSD_HEAD_DIM, axis=-1)
    zeros_lora = jnp.zeros((RWKV_LORA, D_MODEL), F32)
    lw = {
        'norm_mix': _row(P['norm_mix'][l]),
        'w_ssd': jnp.concatenate([z, xbc, _pad_cols(dt, LANE)], axis=1).astype(BF16),
        'ssd_conv_w': P['ssd_conv_w'][l].astype(F32),
        'ssd_conv_b': _row(P['ssd_conv_b'][l]),
        'ssd_dtb_c': _pad_cols(_row(P['ssd_dt_bias'][l]), LANE),
        'ssd_alog_c': _pad_cols(_row(P['ssd_A_log'][l]), LANE),
        'ssd_d_x': _row(rep(P['ssd_D'][l])),
        'ssd_norm': _row(P['ssd_norm'][l]),
        'w_gla': jnp.concatenate([gq, gk, gv, ggate, _pad_cols(glr, LANE)], axis=1).astype(BF16),
        'gla_w2': jnp.pad(P['gla_gk_w2'][l], ((0, LANE - GLA_GATE_RANK), (0, 0))).astype(BF16),
        'gla_gb': _row(P['gla_gk_b'][l]),
        'gla_norm': _row(jnp.tile(P['gla_norm'][l], GLA_HEADS)),
        'w_rwkv': jnp.concatenate([rf, rgate], axis=1).astype(BF16),
        'rwkv_mu': _row(P['rwkv_mu'][l]),
        'rwkv_w0': _row(P['rwkv_w0'][l]),
        'rwkv_w2': jnp.concatenate([P['rwkv_w2'][l], zeros_lora], axis=0).astype(BF16),
        'rwkv_a0': _row(P['rwkv_a0'][l]),
        'rwkv_a2': jnp.concatenate([zeros_lora, P['rwkv_a2'][l]], axis=0).astype(BF16),
        'rwkv_k_k': _row(P['rwkv_k_k'][l]),
        'rwkv_k_a': _row(P['rwkv_k_a'][l]),
        'rwkv_r_k': _row(P['rwkv_r_k'][l]),
        'rwkv_ln_w': _row(P['rwkv_ln_w'][l]),
        'rwkv_ln_b': _row(P['rwkv_ln_b'][l]),
        'w_merge': merge.astype(BF16),
        'b_merge': _row(P['b_merge'][l]),
        'w_proj_ssd': P['w_proj_ssd'][l].astype(BF16),
        'w_proj_gla': P['w_proj_gla'][l].astype(BF16),
        'w_proj_rwkv': P['w_proj_rwkv'][l].astype(BF16),
        'w_out': P['w_out'][l].astype(BF16),
        'norm_xattn': _row(P['norm_xattn'][l]),
        'xa_wq': P['xa_wq'][l].astype(BF16),
        'xa_wo': P['xa_wo'][l].astype(BF16),
    }
    return lw


def _block_rows(L, target):
    return target if L % target == 0 else L


def _trunk(h, mem_k, mem_v, ssd_h, conv_buf, gla_h, rwkv_h, shift_buf, layers, norm_final):
    B, L, _ = h.shape
    tc_ssd = _block_rows(L, SSD_BLOCK_ROWS)
    tc_gla = _block_rows(L, GLA_BLOCK_ROWS)
    tc_rwkv = _block_rows(L, RWKV_BLOCK_ROWS)
    tm = _block_rows(L, MERGE_BLOCK_ROWS)
    new = ([], [], [], [], [])
    for l, lw in enumerate(layers):
        o_ssd, ssd_new, conv_new = _ssd_call(h, lw, ssd_h[l], conv_buf[l], tc_ssd)
        o_gla, gla_new = _gla_call(h, lw, gla_h[l], tc_gla, min(GLA_CHUNK, tc_gla))
        o_rwkv, rwkv_new, shift_new = _rwkv_call(h, lw, rwkv_h[l], shift_buf[l], tc_rwkv, min(RWKV_CHUNK, tc_rwkv))
        h = _merge_call(h, o_ssd, o_gla, o_rwkv, mem_k, mem_v, l, lw, norm_final, tm,
                        final=(l == len(layers) - 1))
        for lst, s_ in zip(new, (ssd_new, conv_new, gla_new, rwkv_new, shift_new)):
            lst.append(s_)
    return h, tuple(jnp.stack(lst, axis=0) for lst in new)


def kernel(x_prompt, x_sample, mem_prompt, state_ssd, state_ssd_conv, state_gla, state_rwkv, state_rwkv_shift,
           cache_mem_k, cache_mem_v, norm_mix, w_in, ssd_conv_w, ssd_conv_b, ssd_dt_bias, ssd_A_log, ssd_D,
           ssd_norm, w_proj_ssd, gla_gk_w2, gla_gk_b, gla_norm, w_proj_gla, rwkv_mu, rwkv_w0, rwkv_w2,
           rwkv_a0, rwkv_a2, rwkv_k_k, rwkv_k_a, rwkv_r_k, rwkv_ln_w, rwkv_ln_b, w_proj_rwkv, b_merge, w_out,
           norm_xattn, xa_wq, xa_wo, norm_mem, xa_wk, xa_wv, norm_final):
    P = dict(norm_mix=norm_mix, w_in=w_in, ssd_conv_w=ssd_conv_w, ssd_conv_b=ssd_conv_b, ssd_dt_bias=ssd_dt_bias,
             ssd_A_log=ssd_A_log, ssd_D=ssd_D, ssd_norm=ssd_norm, w_proj_ssd=w_proj_ssd, gla_gk_w2=gla_gk_w2,
             gla_gk_b=gla_gk_b, gla_norm=gla_norm, w_proj_gla=w_proj_gla, rwkv_mu=rwkv_mu, rwkv_w0=rwkv_w0,
             rwkv_w2=rwkv_w2, rwkv_a0=rwkv_a0, rwkv_a2=rwkv_a2, rwkv_k_k=rwkv_k_k, rwkv_k_a=rwkv_k_a,
             rwkv_r_k=rwkv_r_k, rwkv_ln_w=rwkv_ln_w, rwkv_ln_b=rwkv_ln_b, w_proj_rwkv=w_proj_rwkv,
             b_merge=b_merge, w_out=w_out, norm_xattn=norm_xattn, xa_wq=xa_wq, xa_wo=xa_wo)
    depth = w_in.shape[0]
    layers = [_layer_weights(l, P) for l in range(depth)]
    gf = _row(norm_final)

    bp = mem_prompt.shape[0]
    mem_k_p, mem_v_p = _memkv_call(mem_prompt, norm_mem.reshape(depth, 1, D_MODEL).astype(F32),
                                   xa_wk.astype(BF16), xa_wv.astype(BF16), 256)

    zeros = lambda shape: jnp.zeros((depth, bp) + shape, F32)
    y_prompt, (p_ssd, p_conv, p_gla, p_rwkv, p_shift) = _trunk(
        x_prompt, mem_k_p, mem_v_p,
        zeros((SSD_HEADS, SSD_HEAD_DIM, SSD_STATE)), zeros((SSD_CONV - 1, SSD_CONV_DIM)),
        zeros((GLA_HEADS, GLA_DK, GLA_DV)), zeros((RWKV_HEADS, RWKV_HEAD, RWKV_HEAD)),
        zeros((1, RWKV_SHIFT_COLS)), layers, gf)

    y_sample, (s_ssd, s_conv, s_gla, s_rwkv, s_shift) = _trunk(
        x_sample, cache_mem_k, cache_mem_v,
        state_ssd, state_ssd_conv, state_gla, state_rwkv, state_rwkv_shift, layers, gf)

    return (y_prompt, y_sample, p_ssd, p_conv, p_gla, p_rwkv, p_shift, mem_k_p, mem_v_p,
            s_ssd, s_conv, s_gla, s_rwkv, s_shift)
```

```python
import functools

import jax
import jax.numpy as jnp
import numpy as np
from jax import lax
from jax.experimental import pallas as pl
from jax.experimental.pallas import tpu as pltpu

F32 = jnp.float32
BF16 = jnp.bfloat16

D_MODEL = 1024
NORM_EPS = 1e-5

SSD_HEADS = 16
SSD_HEAD_DIM = 64
SSD_GROUPS = 2
SSD_STATE = 128
SSD_CONV = 4
SSD_CONV_DIM = D_MODEL + 2 * SSD_GROUPS * SSD_STATE
SSD_GROUP_W = D_MODEL // SSD_GROUPS

GLA_HEADS = 4
GLA_DK = 128
GLA_DV = 256
GLA_KEY_DIM = GLA_HEADS * GLA_DK
GLA_GATE_RANK = 16
GLA_GATE_NORMALIZER = 16.0
GLA_CHUNK = 64
GLA_GROUP_CHUNKS = 8

RWKV_HEADS = 16
RWKV_HEAD = 64
RWKV_LORA = 64
RWKV_SHIFT_COLS = 3 * D_MODEL + 2 * RWKV_LORA
RWKV_LN_EPS = 64e-5
RWKV_CHUNK = 64
RWKV_GROUP_CHUNKS = 2

XA_HEADS = 4
XA_HEAD_DIM = 256
N_BRANCHES = 3

LANE = 128
SEG_TILE = 256
VMEM_LIMIT = 56 * 1024 * 1024

SSD_BLOCK_ROWS = 256
GLA_BLOCK_ROWS = 512
RWKV_BLOCK_ROWS = 512
MERGE_BLOCK_ROWS = 512

_IN_SPLITS = (D_MODEL, SSD_CONV_DIM, SSD_HEADS, GLA_KEY_DIM, GLA_KEY_DIM, D_MODEL, D_MODEL,
              GLA_GATE_RANK, RWKV_SHIFT_COLS, D_MODEL, N_BRANCHES * D_MODEL)


def _mm(a, b):
    return jnp.dot(a.astype(BF16), b.astype(BF16), preferred_element_type=F32)


def _mm_nt(a, b):
    return lax.dot_general(a.astype(BF16), b.astype(BF16), (((1,), (1,)), ((), ())),
                           preferred_element_type=F32)


def _mm_tn(a, b):
    return lax.dot_general(a.astype(BF16), b.astype(BF16), (((0,), (0,)), ((), ())),
                           preferred_element_type=F32)


def _split2(x):
    hi = x.astype(BF16)
    lo = (x - hi.astype(F32)).astype(BF16)
    return hi, lo


def _mm_exact_lhs(m_bf16, x):
    hi, lo = _split2(x)
    dot = functools.partial(jnp.dot, preferred_element_type=F32)
    return dot(m_bf16, hi) + dot(m_bf16, lo)


def _mm_exact_rhs(x, m_bf16):
    hi, lo = _split2(x)
    dot = functools.partial(jnp.dot, preferred_element_type=F32)
    return dot(hi, m_bf16) + dot(lo, m_bf16)


def _lower_tri(n, strict=False):
    r = lax.broadcasted_iota(jnp.int32, (n, n), 0)
    c = lax.broadcasted_iota(jnp.int32, (n, n), 1)
    return (r > c) if strict else (r >= c)


def _seg_ones(width, seg):
    r = lax.broadcasted_iota(jnp.int32, (width, width), 0) // seg
    c = lax.broadcasted_iota(jnp.int32, (width, width), 1) // seg
    return jnp.where(r == c, 1.0, 0.0).astype(BF16)


def _head_sums(x, seg):
    ones = _seg_ones(SEG_TILE, seg)
    parts = [jnp.dot(x[:, j:j + SEG_TILE].astype(BF16), ones, preferred_element_type=F32)
             for j in range(0, x.shape[1], SEG_TILE)]
    return jnp.concatenate(parts, axis=1)


def _rms(x, g):
    return x * lax.rsqrt(jnp.mean(x * x, axis=-1, keepdims=True) + NORM_EPS) * g


def _softplus(x):
    return jnp.maximum(x, 0.0) + jnp.log1p(jnp.exp(-jnp.abs(x)))


def _sigmoid(x):
    return 1.0 / (1.0 + jnp.exp(-x))


def _silu(x):
    return x * _sigmoid(x)


def _ssd_kernel(h_ref, g_ref, w_ref, convw_ref, convb_ref, dtbc_ref, alogc_ref, expand_ref,
                dx_ref, gn_ref, st0_ref, conv0_ref,
                o_ref, st_out_ref, conv_out_ref,
                st_scr, xbuf_scr, *, tc):
    c = pl.program_id(1)
    nc = pl.num_programs(1)
    C = SSD_CONV_DIM
    W = D_MODEL

    @pl.when(c == 0)
    def _init():
        st_scr[...] = st0_ref[0].reshape(W, SSD_STATE).T
        xbuf_scr[0:8, :] = jnp.zeros((8, C), F32)
        xbuf_scr[5:8, :] = conv0_ref[0]

    u = _rms(h_ref[0], g_ref[...]).astype(BF16)
    z = jnp.dot(u, w_ref[:, 0:W], preferred_element_type=F32)
    xbuf_scr[8:8 + tc, :] = jnp.dot(u, w_ref[:, W:W + C], preferred_element_type=F32)
    dtc_raw = jnp.dot(u, w_ref[:, W + C:W + C + LANE], preferred_element_type=F32)

    conv = convb_ref[...] + convw_ref[SSD_CONV - 1:SSD_CONV, :] * xbuf_scr[8:8 + tc, :]
    for j in range(SSD_CONV - 1):
        conv = conv + convw_ref[j:j + 1, :] * xbuf_scr[5 + j:5 + j + tc, :]
    conv_out_ref[0] = xbuf_scr[tc + 5:tc + 8, :]
    xbuf_scr[0:8, :] = xbuf_scr[tc:tc + 8, :]

    xbc = _silu(conv)
    xs = xbc[:, 0:W]
    bm = xbc[:, W:W + SSD_GROUPS * SSD_STATE].astype(BF16)
    cm = xbc[:, W + SSD_GROUPS * SSD_STATE:C].astype(BF16)

    dtc = _softplus(dtc_raw + dtbc_ref[...])
    tri = jnp.where(_lower_tri(tc), 1.0, 0.0).astype(BF16)
    acs_c = _mm_exact_lhs(tri, dtc * -jnp.exp(alogc_ref[...]))
    dtx = _mm_exact_rhs(dtc, expand_ref[...])
    acs_x = _mm_exact_rhs(acs_c, expand_ref[...])
    acs_ct = acs_c.T
    last = acs_x[tc - 1:tc, :]
    e_acs = jnp.exp(acs_x)
    xd = xs * dtx
    xd_b = xd.astype(BF16)
    xd_end = (xd * jnp.exp(last - acs_x)).astype(BF16)
    chunk_decay = jnp.exp(last)
    causal = _lower_tri(tc)

    ys = []
    for g in range(SSD_GROUPS):
        gs = slice(g * SSD_GROUP_W, (g + 1) * SSD_GROUP_W)
        bg = bm[:, g * SSD_STATE:(g + 1) * SSD_STATE]
        cg = cm[:, g * SSD_STATE:(g + 1) * SSD_STATE]
        cb = _mm_nt(cg, bg)
        st_g = st_scr[:, gs]
        y_off = _mm(cg, st_g) * e_acs[:, gs]
        parts = []
        for e in range(SSD_HEADS // SSD_GROUPS):
            hh = g * (SSD_HEADS // SSD_GROUPS) + e
            seg = acs_c[:, hh:hh + 1] - acs_ct[hh:hh + 1, :]
            lmat = jnp.exp(jnp.where(causal, seg, -jnp.inf))
            parts.append(_mm(cb * lmat, xd_b[:, hh * SSD_HEAD_DIM:(hh + 1) * SSD_HEAD_DIM]))
        ys.append(jnp.concatenate(parts, axis=1) + y_off)
        st_scr[:, gs] = st_g * chunk_decay[:, gs] + _mm_tn(bg, xd_end[:, gs])
    y = jnp.concatenate(ys, axis=1) + xs * dx_ref[...]

    t = y * _silu(z)
    outs = []
    for g in range(SSD_GROUPS):
        tg = t[:, g * SSD_GROUP_W:(g + 1) * SSD_GROUP_W]
        outs.append(tg * lax.rsqrt(jnp.mean(tg * tg, axis=-1, keepdims=True) + NORM_EPS))
    o_ref[0] = (jnp.concatenate(outs, axis=1) * gn_ref[...]).astype(o_ref.dtype)

    @pl.when(c == nc - 1)
    def _fin():
        st_out_ref[0] = st_scr[...].T.reshape(SSD_HEADS, SSD_HEAD_DIM, SSD_STATE)


def _const_spec(shape):
    nd = len(shape)
    return pl.BlockSpec(shape, lambda b, c: (0,) * nd, pipeline_mode=pl.Buffered(1))


def _ssd_call(h, lw, st0, conv0, tc):
    B, L, _ = h.shape
    nc = L // tc
    expand = (lax.broadcasted_iota(jnp.int32, (LANE, D_MODEL), 1) // SSD_HEAD_DIM
              == lax.broadcasted_iota(jnp.int32, (LANE, D_MODEL), 0)).astype(BF16)
    consts = [lw['norm_mix'], lw['w_ssd'], lw['ssd_conv_w'], lw['ssd_conv_b'],
              lw['ssd_dtb_c'], lw['ssd_alog_c'], expand, lw['ssd_d_x'], lw['ssd_norm']]
    in_specs = ([pl.BlockSpec((1, tc, D_MODEL), lambda b, c: (b, c, 0))]
                + [_const_spec(a.shape) for a in consts]
                + [pl.BlockSpec((1, SSD_HEADS, SSD_HEAD_DIM, SSD_STATE), lambda b, c: (b, 0, 0, 0)),
                   pl.BlockSpec((1, SSD_CONV - 1, SSD_CONV_DIM), lambda b, c: (b, 0, 0))])
    out_specs = [pl.BlockSpec((1, tc, D_MODEL), lambda b, c: (b, c, 0)),
                 pl.BlockSpec((1, SSD_HEADS, SSD_HEAD_DIM, SSD_STATE), lambda b, c: (b, 0, 0, 0)),
                 pl.BlockSpec((1, SSD_CONV - 1, SSD_CONV_DIM), lambda b, c: (b, 0, 0))]
    out_shape = [jax.ShapeDtypeStruct((B, L, D_MODEL), BF16),
                 jax.ShapeDtypeStruct((B, SSD_HEADS, SSD_HEAD_DIM, SSD_STATE), F32),
                 jax.ShapeDtypeStruct((B, SSD_CONV - 1, SSD_CONV_DIM), F32)]
    return pl.pallas_call(
        functools.partial(_ssd_kernel, tc=tc),
        grid=(B, nc), in_specs=in_specs, out_specs=out_specs, out_shape=out_shape,
        scratch_shapes=[pltpu.VMEM((SSD_STATE, D_MODEL), F32), pltpu.VMEM((tc + 8, SSD_CONV_DIM), F32)],
        compiler_params=pltpu.CompilerParams(dimension_semantics=("arbitrary", "arbitrary"),
                                             vmem_limit_bytes=VMEM_LIMIT),
        name="ssd_mixer",
    )(h, *consts, st0, conv0)


def _gla_kernel(h_ref, g_ref, w_ref, w2_ref, gb_ref, gn_ref, s0_ref,
                o_ref, s_out_ref,
                s_scr, o_scr, *, tc, q):
    c = pl.program_id(1)
    nc = pl.num_programs(1)
    KD = GLA_KEY_DIM
    W = D_MODEL

    @pl.when(c == 0)
    def _init():
        for hd in range(GLA_HEADS):
            s_scr[hd] = s0_ref[0, hd].T

    u = _rms(h_ref[0], g_ref[...]).astype(BF16)
    qf = jnp.dot(u, w_ref[:, 0:KD], preferred_element_type=F32) * (GLA_DK ** -0.5)
    kf = jnp.dot(u, w_ref[:, KD:2 * KD], preferred_element_type=F32)
    vf = jnp.dot(u, w_ref[:, 2 * KD:2 * KD + W], preferred_element_type=F32).astype(BF16)
    glr = jnp.dot(u, w_ref[:, 2 * KD + 2 * W:2 * KD + 2 * W + LANE], preferred_element_type=F32)
    gpre = _mm(glr, w2_ref[...]) + gb_ref[...]
    gl = -_softplus(-gpre) / GLA_GATE_NORMALIZER

    rr = lax.broadcasted_iota(jnp.int32, (tc, tc), 0)
    cc = lax.broadcasted_iota(jnp.int32, (tc, tc), 1)
    tri = jnp.where((rr // q == cc // q) & (rr >= cc), 1.0, 0.0).astype(BF16)
    gcs = _mm_exact_lhs(tri, gl)
    causal = _lower_tri(q)

    def chunk_units(s):
        rows = slice(s * q, (s + 1) * q)
        g = gcs[rows]
        last = g[q - 1:q, :]
        kk = kf[rows]
        qe = (qf[rows] * jnp.exp(g)).astype(BF16)
        ke = (kk * jnp.exp(-g)).astype(BF16)
        kend = (kk * jnp.exp(last - g)).astype(BF16)
        dec = jnp.exp(last)
        out = []
        for hd in range(GLA_HEADS):
            ks = slice(hd * GLA_DK, (hd + 1) * GLA_DK)
            vs = slice(hd * GLA_DV, (hd + 1) * GLA_DV)
            out.append(dict(rows=rows, vs=vs, qe=qe[:, ks], ke=ke[:, ks], kend=kend[:, ks], dec=dec[:, ks],
                            v=vf[rows, vs]))
        return out

    states = [s_scr[hd] for hd in range(GLA_HEADS)]
    for s0 in range(0, tc // q, GLA_GROUP_CHUNKS):
        group = [un for s in range(s0, min(s0 + GLA_GROUP_CHUNKS, tc // q)) for un in chunk_units(s)]
        amats = [jnp.where(causal, _mm_nt(un['qe'], un['ke']), 0.0) for un in group]
        intra = [_mm(a, un['v']) for a, un in zip(amats, group)]
        chunk_states = [_mm_tn(un['v'], un['kend']) for un in group]
        for j in range(0, len(group), GLA_HEADS):
            for hd in range(GLA_HEADS):
                un = group[j + hd]
                o_scr[un['rows'], un['vs']] = intra[j + hd] + _mm_nt(un['qe'], states[hd])
            states = [st * group[j + hd]['dec'] + chunk_states[j + hd] for hd, st in enumerate(states)]
    for hd in range(GLA_HEADS):
        s_scr[hd] = states[hd]

    gate = _silu(jnp.dot(u, w_ref[:, 2 * KD + W:2 * KD + 2 * W], preferred_element_type=F32))
    o = o_scr[...]
    outs = []
    for hd in range(GLA_HEADS):
        oh = o[:, hd * GLA_DV:(hd + 1) * GLA_DV]
        outs.append(oh * lax.rsqrt(jnp.mean(oh * oh, axis=-1, keepdims=True) + NORM_EPS))
    o_ref[0] = (jnp.concatenate(outs, axis=1) * gn_ref[...] * gate).astype(o_ref.dtype)

    @pl.when(c == nc - 1)
    def _fin():
        for hd in range(GLA_HEADS):
            s_out_ref[0, hd] = s_scr[hd].T


def _gla_call(h, lw, s0, tc, q):
    B, L, _ = h.shape
    nc = L // tc
    consts = [lw['norm_mix'], lw['w_gla'], lw['gla_w2'], lw['gla_gb'], lw['gla_norm']]
    in_specs = ([pl.BlockSpec((1, tc, D_MODEL), lambda b, c: (b, c, 0))]
                + [_const_spec(a.shape) for a in consts]
                + [pl.BlockSpec((1, GLA_HEADS, GLA_DK, GLA_DV), lambda b, c: (b, 0, 0, 0))])
    out_specs = [pl.BlockSpec((1, tc, D_MODEL), lambda b, c: (b, c, 0)),
                 pl.BlockSpec((1, GLA_HEADS, GLA_DK, GLA_DV), lambda b, c: (b, 0, 0, 0))]
    out_shape = [jax.ShapeDtypeStruct((B, L, D_MODEL), BF16),
                 jax.ShapeDtypeStruct((B, GLA_HEADS, GLA_DK, GLA_DV), F32)]
    return pl.pallas_call(
        functools.partial(_gla_kernel, tc=tc, q=q),
        grid=(B, nc), in_specs=in_specs, out_specs=out_specs, out_shape=out_shape,
        scratch_shapes=[pltpu.VMEM((GLA_HEADS, GLA_DV, GLA_DK), F32), pltpu.VMEM((tc, D_MODEL), F32)],
        compiler_params=pltpu.CompilerParams(dimension_semantics=("arbitrary", "arbitrary"),
                                             vmem_limit_bytes=VMEM_LIMIT),
        name="gla_mixer",
    )(h, *consts, s0)


def _rwkv_kernel(h_ref, g_ref, w_ref, mu_ref, w0_ref, w2_ref, a0_ref, a2_ref, kk_ref, ka_ref, rk_ref,
                 lnw_ref, lnb_ref, s0_ref, sh0_ref,
                 o_ref, s_out_ref, sh_out_ref,
                 s_scr, carry_scr, o_scr, *, tc, q):
    c = pl.program_id(1)
    nc = pl.num_programs(1)
    W = D_MODEL
    N = RWKV_HEAD

    @pl.when(c == 0)
    def _init():
        for hd in range(RWKV_HEADS):
            s_scr[:, hd * N:(hd + 1) * N] = s0_ref[0, hd]
        carry_scr[...] = sh0_ref[0]

    u = _rms(h_ref[0], g_ref[...]).astype(BF16)
    rf = jnp.dot(u, w_ref[:, 0:RWKV_SHIFT_COLS], preferred_element_type=F32)
    first = lax.broadcasted_iota(jnp.int32, (tc, 1), 0) == 0
    prev = jnp.where(first, carry_scr[...], pltpu.roll(rf, 1, axis=0))
    carry_scr[...] = rf[tc - 1:tc, :]
    sh_out_ref[0] = rf[tc - 1:tc, :]
    rfm = rf + (prev - rf) * mu_ref[...]
    r7 = rfm[:, 0:W]
    k7 = rfm[:, W:2 * W]
    v7 = rfm[:, 2 * W:3 * W]
    lora = rfm[:, 3 * W:3 * W + 2 * RWKV_LORA]
    w_pre = w0_ref[...] + _mm(jnp.tanh(lora), w2_ref[...])
    a = _sigmoid(a0_ref[...] + _mm(lora, a2_ref[...]))
    kkf = k7 * kk_ref[...]
    kk = kkf / jnp.maximum(jnp.sqrt(_head_sums(kkf * kkf, N)), 1e-12)
    k7 = k7 * (1.0 + (a - 1.0) * ka_ref[...])
    bonus = _head_sums(r7 * k7 * rk_ref[...], N) * v7
    pv = -kk
    qv = kk * a
    lw = _sigmoid(w_pre) * (-float(np.exp(-0.5)))

    rr = lax.broadcasted_iota(jnp.int32, (tc, tc), 0)
    cc = lax.broadcasted_iota(jnp.int32, (tc, tc), 1)
    tri = jnp.where((rr // q == cc // q) & (rr >= cc), 1.0, 0.0).astype(BF16)
    cum = _mm_exact_lhs(tri, lw)
    steps = max(1, (q - 1).bit_length())
    PW = 2 * N

    def left(x):
        return jnp.where(lax.broadcasted_iota(jnp.int32, x.shape, 1) < N, x, jnp.zeros_like(x))

    def right(x):
        return jnp.where(lax.broadcasted_iota(jnp.int32, x.shape, 1) >= N, x, jnp.zeros_like(x))

    def pick(a_src, b_src):
        return jnp.where(lax.broadcasted_iota(jnp.int32, a_src.shape, 1) < a_src.shape[1] // 2, a_src, b_src)

    row4 = lax.broadcasted_iota(jnp.int32, (q, 4 * q), 0)
    col4 = lax.broadcasted_iota(jnp.int32, (q, 4 * q), 1)
    blk4 = col4 // q
    tcol4 = col4 - blk4 * q
    outer4 = (blk4 == 0) | (blk4 == 3)
    incl4 = row4 >= tcol4
    strict_k4 = (~outer4) & (row4 > tcol4)
    strict_q4 = outer4 & (row4 > tcol4)
    eye_x4 = (~outer4) & (row4 == tcol4)

    npairs = RWKV_HEADS // 2
    tiles = [slice(t * PW, (t + 1) * PW) for t in range(npairs)]
    zq = jnp.zeros((q, PW), BF16)
    decs = []

    def chunk_units(s):
        rows = slice(s * q, (s + 1) * q)
        cum_s = cum[rows]
        last = cum_s[q - 1:q, :]
        e_inv = jnp.exp(-cum_s)
        e_end = jnp.exp(last - cum_s)
        rt = r7[rows] * jnp.exp(cum_s)
        pt = pv[rows] * jnp.exp(cum_s - lw[rows])
        rt_b, pt_b = rt.astype(BF16), pt.astype(BF16)
        qt = (qv[rows] * e_inv).astype(BF16)
        kt = (k7[rows] * e_inv).astype(BF16)
        qend = (qv[rows] * e_end).astype(BF16)
        kend = (k7[rows] * e_end).astype(BF16)
        vv = v7[rows].astype(BF16)
        decs.append(jnp.exp(last))
        out = []
        for ts in tiles:
            vt = vv[:, ts]
            out.append(dict(
                rows=rows, ts=ts, rt=rt[:, ts], pt=pt[:, ts],
                pr=jnp.concatenate([pt_b[:, ts], rt_b[:, ts]], axis=0),
                qk4=jnp.concatenate([left(qt[:, ts]), left(kt[:, ts]), right(kt[:, ts]), right(qt[:, ts])], axis=0),
                ends4=jnp.concatenate([left(qend[:, ts]), left(kend[:, ts]),
                                       right(kend[:, ts]), right(qend[:, ts])], axis=0),
                va=left(vt), vb=right(vt)))
        return out
    zh = jnp.zeros((q, 2 * q), BF16)

    def local_terms(group):
        amats = [_mm_nt(un['pr'], un['qk4']) for un in group]
        zvs = [_mm(jnp.where(strict_k4, am[0:q], 0.0),
                   jnp.concatenate([un['va'], un['va'], un['vb'], un['vb']], axis=0)) for am, un in zip(amats, group)]
        ys = [jnp.where(strict_q4, am[0:q], 0.0) + jnp.where(eye_x4, 1.0, 0.0) for am in amats]
        for _ in range(steps):
            pys = []
            for y in ys:
                t0, t1 = y[:, 0:2 * q].astype(BF16), y[:, 2 * q:4 * q].astype(BF16)
                rhs = jnp.concatenate([jnp.concatenate([t0, zh], axis=1), jnp.concatenate([zh, t1], axis=1)], axis=0)
                pys.append(_mm(pick(t0, t1), rhs))
            ys = [py + jnp.where(outer4, 0.0, y) for py, y in zip(pys, ys)]
        sols = []
        for y, zv, un in zip(ys, zvs, group):
            pt_sw = pltpu.roll(un['pt'], N, axis=1)
            r_a = pick(zv, pt_sw).astype(BF16)
            r_b = pick(pt_sw, zv).astype(BF16)
            rhs = jnp.concatenate([jnp.concatenate([zq, r_b], axis=1), jnp.concatenate([r_a, zq], axis=1)], axis=0)
            sols.append(_mm(pick(y[:, 2 * q:4 * q], y[:, 0:2 * q]), rhs).astype(BF16))
        x4s = [jnp.concatenate([jnp.concatenate([sol[:, 0:PW], zq], axis=1),
                                jnp.concatenate([un['va'], zq], axis=1),
                                jnp.concatenate([zq, un['vb']], axis=1),
                                jnp.concatenate([zq, sol[:, PW:2 * PW]], axis=1)], axis=0)
               for sol, un in zip(sols, group)]
        return ([_mm(jnp.where(incl4, am[q:2 * q], 0.0), x4) for am, x4 in zip(amats, x4s)],
                [_mm_tn(x4, un['ends4']) for x4, un in zip(x4s, group)])

    def advance(states, s, group, ows, sms):
        for t, ts in enumerate(tiles):
            ow0, ow1 = ows[t][:, 0:PW], ows[t][:, PW:2 * PW]
            r_eff = group[t]['rt'] + pltpu.roll(pick(ow1, ow0), N, axis=1)
            bd = jnp.concatenate([left(states[t]), right(states[t])], axis=0)
            o_scr[group[t]['rows'], ts] = pick(ow0, ow1) + _mm_nt(r_eff, bd)
        nxt = []
        for t, ts in enumerate(tiles):
            sm = sms[t]
            m_t = sm[N:2 * N] + sm[2 * N:3 * N]
            mbd = jnp.concatenate([left(m_t), right(m_t)], axis=0)
            nxt.append(states[t] * decs[s][:, ts] + _mm(states[t], mbd) + (sm[0:N] + sm[3 * N:4 * N]))
        return nxt

    states = [s_scr[:, ts] for ts in tiles]
    for s0 in range(0, tc // q, RWKV_GROUP_CHUNKS):
        chunks = range(s0, min(s0 + RWKV_GROUP_CHUNKS, tc // q))
        group = [un for s in chunks for un in chunk_units(s)]
        o_g, s_g = local_terms(group)
        for j, s in enumerate(chunks):
            sl = slice(j * npairs, (j + 1) * npairs)
            states = advance(states, s, group[sl], o_g[sl], s_g[sl])
    for t, ts in enumerate(tiles):
        s_scr[:, ts] = states[t]

    o7 = o_scr[...]
    mean = _head_sums(o7, N) * (1.0 / N)
    d = o7 - mean
    var = _head_sums(d * d, N) * (1.0 / N)
    ln = d * lax.rsqrt(var + RWKV_LN_EPS) * lnw_ref[...] + lnb_ref[...]
    gate = _silu(jnp.dot(u, w_ref[:, RWKV_SHIFT_COLS:RWKV_SHIFT_COLS + W], preferred_element_type=F32))
    o_ref[0] = ((ln + bonus) * gate).astype(o_ref.dtype)

    @pl.when(c == nc - 1)
    def _fin():
        for hd in range(RWKV_HEADS):
            s_out_ref[0, hd] = s_scr[:, hd * N:(hd + 1) * N]


def _rwkv_call(h, lw, s0, sh0, tc, q):
    B, L, _ = h.shape
    nc = L // tc
    consts = [lw['norm_mix'], lw['w_rwkv'], lw['rwkv_mu'], lw['rwkv_w0'], lw['rwkv_w2'], lw['rwkv_a0'], lw['rwkv_a2'],
              lw['rwkv_k_k'], lw['rwkv_k_a'], lw['rwkv_r_k'], lw['rwkv_ln_w'], lw['rwkv_ln_b']]
    in_specs = ([pl.BlockSpec((1, tc, D_MODEL), lambda b, c: (b, c, 0))]
                + [_const_spec(a.shape) for a in consts]
                + [pl.BlockSpec((1, RWKV_HEADS, RWKV_HEAD, RWKV_HEAD), lambda b, c: (b, 0, 0, 0)),
                   pl.BlockSpec((1, 1, RWKV_SHIFT_COLS), lambda b, c: (b, 0, 0))])
    out_specs = [pl.BlockSpec((1, tc, D_MODEL), lambda b, c: (b, c, 0)),
                 pl.BlockSpec((1, RWKV_HEADS, RWKV_HEAD, RWKV_HEAD), lambda b, c: (b, 0, 0, 0)),
                 pl.BlockSpec((1, 1, RWKV_SHIFT_COLS), lambda b, c: (b, 0, 0))]
    out_shape = [jax.ShapeDtypeStruct((B, L, D_MODEL), BF16),
                 jax.ShapeDtypeStruct((B, RWKV_HEADS, RWKV_HEAD, RWKV_HEAD), F32),
                 jax.ShapeDtypeStruct((B, 1, RWKV_SHIFT_COLS), F32)]
    return pl.pallas_call(
        functools.partial(_rwkv_kernel, tc=tc, q=q),
        grid=(B, nc), in_specs=in_specs, out_specs=out_specs, out_shape=out_shape,
        scratch_shapes=[pltpu.VMEM((RWKV_HEAD, D_MODEL), F32),
                        pltpu.VMEM((1, RWKV_SHIFT_COLS), F32),
                        pltpu.VMEM((tc, D_MODEL), F32)],
        compiler_params=pltpu.CompilerParams(dimension_semantics=("arbitrary", "arbitrary"),
                                             vmem_limit_bytes=VMEM_LIMIT),
        name="rwkv_mixer",
    )(h, *consts, s0, sh0)


def _merge_kernel(h_ref, os_ref, og_ref, or_ref, mk_ref, mv_ref,
                  g_ref, wm_ref, bm_ref, ps_ref, pg_ref, pr_ref, wo_ref, gx_ref, xq_ref, xo_ref, gf_ref,
                  out_ref, k_scr, v_scr, *, final):
    W = D_MODEL

    @pl.when(pl.program_id(1) == 0)
    def _stage_memory():
        for hd in range(XA_HEADS):
            k_scr[hd] = mk_ref[0, 0, :, hd, :].astype(BF16)
            v_scr[hd] = mv_ref[0, 0, :, hd, :].astype(BF16)

    h = h_ref[0]
    u = _rms(h, g_ref[...]).astype(BF16)
    m = None
    for i, (branch_ref, proj_ref) in enumerate(((os_ref, ps_ref), (og_ref, pg_ref), (or_ref, pr_ref))):
        s = _sigmoid(jnp.dot(u, wm_ref[:, i * W:(i + 1) * W], preferred_element_type=F32)
                     + bm_ref[:, i * W:(i + 1) * W])
        term = s * jnp.dot(branch_ref[0], proj_ref[...], preferred_element_type=F32)
        m = term if m is None else m + term
    h1 = h + _mm(m, wo_ref[...])

    u2 = _rms(h1, gx_ref[...])
    qx = _mm(u2, xq_ref[...])
    outs = []
    for hd in range(XA_HEADS):
        hs = slice(hd * XA_HEAD_DIM, (hd + 1) * XA_HEAD_DIM)
        sc = _mm_nt(qx[:, hs], k_scr[hd]) * (XA_HEAD_DIM ** -0.5)
        sc = sc - jnp.max(sc, axis=-1, keepdims=True)
        e = jnp.exp(sc)
        p = e / jnp.sum(e, axis=-1, keepdims=True)
        outs.append(_mm(p, v_scr[hd]))
    h2 = h1 + _mm(jnp.concatenate(outs, axis=1), xo_ref[...])
    out_ref[0] = _rms(h2, gf_ref[...]) if final else h2


def _merge_call(h, o_ssd, o_gla, o_rwkv, mem_k, mem_v, layer, lw, norm_final, tm, final):
    B, L, _ = h.shape
    M = mem_k.shape[2]
    consts = [lw['norm_mix'], lw['w_merge'], lw['b_merge'], lw['w_proj_ssd'], lw['w_proj_gla'], lw['w_proj_rwkv'],
              lw['w_out'], lw['norm_xattn'], lw['xa_wq'], lw['xa_wo'], norm_final]
    tok = pl.BlockSpec((1, tm, D_MODEL), lambda b, c: (b, c, 0))
    mem = pl.BlockSpec((1, 1, M, XA_HEADS, XA_HEAD_DIM), lambda b, c: (layer, b, 0, 0, 0))
    return pl.pallas_call(
        functools.partial(_merge_kernel, final=final),
        grid=(B, L // tm),
        in_specs=[tok, tok, tok, tok, mem, mem] + [_const_spec(a.shape) for a in consts],
        out_specs=tok,
        out_shape=jax.ShapeDtypeStruct((B, L, D_MODEL), F32),
        scratch_shapes=[pltpu.VMEM((XA_HEADS, M, XA_HEAD_DIM), BF16), pltpu.VMEM((XA_HEADS, M, XA_HEAD_DIM), BF16)],
        compiler_params=pltpu.CompilerParams(dimension_semantics=("arbitrary", "arbitrary"),
                                             vmem_limit_bytes=VMEM_LIMIT),
        name="merge_xattn",
    )(h, o_ssd, o_gla, o_rwkv, mem_k, mem_v, *consts)


def _memkv_kernel(x_ref, g_ref, wk_ref, wv_ref, k_ref, v_ref):
    u = _rms(x_ref[...], g_ref[0]).astype(BF16)
    k = jnp.dot(u, wk_ref[0], preferred_element_type=F32)
    v = jnp.dot(u, wv_ref[0], preferred_element_type=F32)
    for hd in range(XA_HEADS):
        hs = slice(hd * XA_HEAD_DIM, (hd + 1) * XA_HEAD_DIM)
        k_ref[0, :, hd, :] = k[:, hs]
        v_ref[0, :, hd, :] = v[:, hs]


def _memkv_call(mem, g, wk, wv, tm):
    B, M, _ = mem.shape
    depth = wk.shape[0]
    x = mem.reshape(B * M, D_MODEL)
    rows = pl.BlockSpec((tm, D_MODEL), lambda l, i: (i, 0))
    per_layer = lambda a: pl.BlockSpec((1,) + a.shape[1:], lambda l, i: (l,) + (0,) * (a.ndim - 1))
    out = pl.BlockSpec((1, tm, XA_HEADS, XA_HEAD_DIM), lambda l, i: (l, i, 0, 0))
    k, v = pl.pallas_call(
        _memkv_kernel,
        grid=(depth, B * M // tm),
        in_specs=[rows, per_layer(g), per_layer(wk), per_layer(wv)],
        out_specs=[out, out],
        out_shape=[jax.ShapeDtypeStruct((depth, B * M, XA_HEADS, XA_HEAD_DIM), F32)] * 2,
        compiler_params=pltpu.CompilerParams(dimension_semantics=("arbitrary", "arbitrary"),
                                             vmem_limit_bytes=VMEM_LIMIT),
        name="mem_kv",
    )(x, g, wk, wv)
    shape = (depth, B, M, XA_HEADS, XA_HEAD_DIM)
    return k.reshape(shape), v.reshape(shape)


def _pad_cols(w, n):
    return jnp.pad(w, ((0, 0), (0, n - w.shape[1])))


def _row(v):
    return v.reshape(1, -1).astype(F32)


def _layer_weights(l, P):
    w_in = P['w_in'][l]
    offs = [0]
    for s in _IN_SPLITS:
        offs.append(offs[-1] + s)
    z, xbc, dt, gq, gk, gv, ggate, glr, rf, rgate, merge = (w_in[:, offs[i]:offs[i + 1]] for i in range(len(_IN_SPLITS)))
    rep = lambda v: jnp.repeat(v, SSD_HEAD_DIM, axis=-1)
    zeros_lora = jnp.zeros((RWKV_LORA, D_MODEL), F32)
    lw = {
        'norm_mix': _row(P['norm_mix'][l]),
        'w_ssd': jnp.concatenate([z, xbc, _pad_cols(dt, LANE)], axis=1).astype(BF16),
        'ssd_conv_w': P['ssd_conv_w'][l].astype(F32),
        'ssd_conv_b': _row(P['ssd_conv_b'][l]),
        'ssd_dtb_c': _pad_cols(_row(P['ssd_dt_bias'][l]), LANE),
        'ssd_alog_c': _pad_cols(_row(P['ssd_A_log'][l]), LANE),
        'ssd_d_x': _row(rep(P['ssd_D'][l])),
        'ssd_norm': _row(P['ssd_norm'][l]),
        'w_gla': jnp.concatenate([gq, gk, gv, ggate, _pad_cols(glr, LANE)], axis=1).astype(BF16),
        'gla_w2': jnp.pad(P['gla_gk_w2'][l], ((0, LANE - GLA_GATE_RANK), (0, 0))).astype(BF16),
        'gla_gb': _row(P['gla_gk_b'][l]),
        'gla_norm': _row(jnp.tile(P['gla_norm'][l], GLA_HEADS)),
        'w_rwkv': jnp.concatenate([rf, rgate], axis=1).astype(BF16),
        'rwkv_mu': _row(P['rwkv_mu'][l]),
        'rwkv_w0': _row(P['rwkv_w0'][l]),
        'rwkv_w2': jnp.concatenate([P['rwkv_w2'][l], zeros_lora], axis=0).astype(BF16),
        'rwkv_a0': _row(P['rwkv_a0'][l]),
        'rwkv_a2': jnp.concatenate([zeros_lora, P['rwkv_a2'][l]], axis=0).astype(BF16),
        'rwkv_k_k': _row(P['rwkv_k_k'][l]),
        'rwkv_k_a': _row(P['rwkv_k_a'][l]),
        'rwkv_r_k': _row(P['rwkv_r_k'][l]),
        'rwkv_ln_w': _row(P['rwkv_ln_w'][l]),
        'rwkv_ln_b': _row(P['rwkv_ln_b'][l]),
        'w_merge': merge.astype(BF16),
        'b_merge': _row(P['b_merge'][l]),
        'w_proj_ssd': P['w_proj_ssd'][l].astype(BF16),
        'w_proj_gla': P['w_proj_gla'][l].astype(BF16),
        'w_proj_rwkv': P['w_proj_rwkv'][l].astype(BF16),
        'w_out': P['w_out'][l].astype(BF16),
        'norm_xattn': _row(P['norm_xattn'][l]),
        'xa_wq': P['xa_wq'][l].astype(BF16),
        'xa_wo': P['xa_wo'][l].astype(BF16),
    }
    return lw


def _block_rows(L, target):
    return target if L % target == 0 else L


def _trunk(h, mem_k, mem_v, ssd_h, conv_buf, gla_h, rwkv_h, shift_buf, layers, norm_final):
    B, L, _ = h.shape
    tc_ssd = _block_rows(L, SSD_BLOCK_ROWS)
    tc_gla = _block_rows(L, GLA_BLOCK_ROWS)
    tc_rwkv = _block_rows(L, RWKV_BLOCK_ROWS)
    tm = _block_rows(L, MERGE_BLOCK_ROWS)
    new = ([], [], [], [], [])
    for l, lw in enumerate(layers):
        o_ssd, ssd_new, conv_new = _ssd_call(h, lw, ssd_h[l], conv_buf[l], tc_ssd)
        o_gla, gla_new = _gla_call(h, lw, gla_h[l], tc_gla, min(GLA_CHUNK, tc_gla))
        o_rwkv, rwkv_new, shift_new = _rwkv_call(h, lw, rwkv_h[l], shift_buf[l], tc_rwkv, min(RWKV_CHUNK, tc_rwkv))
        h = _merge_call(h, o_ssd, o_gla, o_rwkv, mem_k, mem_v, l, lw, norm_final, tm,
                        final=(l == len(layers) - 1))
        for lst, s_ in zip(new, (ssd_new, conv_new, gla_new, rwkv_new, shift_new)):
            lst.append(s_)
    return h, tuple(jnp.stack(lst, axis=0) for lst in new)


def kernel(x_prompt, x_sample, mem_prompt, state_ssd, state_ssd_conv, state_gla, state_rwkv, state_rwkv_shift,
           cache_mem_k, cache_mem_v, norm_mix, w_in, ssd_conv_w, ssd_conv_b, ssd_dt_bias, ssd_A_log, ssd_D,
           ssd_norm, w_proj_ssd, gla_gk_w2, gla_gk_b, gla_norm, w_proj_gla, rwkv_mu, rwkv_w0, rwkv_w2,
           rwkv_a0, rwkv_a2, rwkv_k_k, rwkv_k_a, rwkv_r_k, rwkv_ln_w, rwkv_ln_b, w_proj_rwkv, b_merge, w_out,
           norm_xattn, xa_wq, xa_wo, norm_mem, xa_wk, xa_wv, norm_final):
    P = dict(norm_mix=norm_mix, w_in=w_in, ssd_conv_w=ssd_conv_w, ssd_conv_b=ssd_conv_b, ssd_dt_bias=ssd_dt_bias,
             ssd_A_log=ssd_A_log, ssd_D=ssd_D, ssd_norm=ssd_norm, w_proj_ssd=w_proj_ssd, gla_gk_w2=gla_gk_w2,
             gla_gk_b=gla_gk_b, gla_norm=gla_norm, w_proj_gla=w_proj_gla, rwkv_mu=rwkv_mu, rwkv_w0=rwkv_w0,
             rwkv_w2=rwkv_w2, rwkv_a0=rwkv_a0, rwkv_a2=rwkv_a2, rwkv_k_k=rwkv_k_k, rwkv_k_a=rwkv_k_a,
             rwkv_r_k=rwkv_r_k, rwkv_ln_w=rwkv_ln_w, rwkv_ln_b=rwkv_ln_b, w_proj_rwkv=w_proj_rwkv,
             b_merge=b_merge, w_out=w_out, norm_xattn=norm_xattn, xa_wq=xa_wq, xa_wo=xa_wo)
    depth = w_in.shape[0]
    layers = [_layer_weights(l, P) for l in range(depth)]
    gf = _row(norm_final)

    bp = mem_prompt.shape[0]
    mem_k_p, mem_v_p = _memkv_call(mem_prompt, norm_mem.reshape(depth, 1, D_MODEL).astype(F32),
                                   xa_wk.astype(BF16), xa_wv.astype(BF16), 256)

    zeros = lambda shape: jnp.zeros((depth, bp) + shape, F32)
    y_prompt, (p_ssd, p_conv, p_gla, p_rwkv, p_shift) = _trunk(
        x_prompt, mem_k_p, mem_v_p,
        zeros((SSD_HEADS, SSD_HEAD_DIM, SSD_STATE)), zeros((SSD_CONV - 1, SSD_CONV_DIM)),
        zeros((GLA_HEADS, GLA_DK, GLA_DV)), zeros((RWKV_HEADS, RWKV_HEAD, RWKV_HEAD)),
        zeros((1, RWKV_SHIFT_COLS)), layers, gf)

    y_sample, (s_ssd, s_conv, s_gla, s_rwkv, s_shift) = _trunk(
        x_sample, cache_mem_k, cache_mem_v,
        state_ssd, state_ssd_conv, state_gla, state_rwkv, state_rwkv_shift, layers, gf)

    return (y_prompt, y_sample, p_ssd, p_conv, p_gla, p_rwkv, p_shift, mem_k_p, mem_v_p,
            s_ssd, s_conv, s_gla, s_rwkv, s_shift)
```
